```python
import math
import jax, jax.numpy as jnp
from jax import lax
import numpy as np

D_MODEL = 1024
BATCH = 4
SEQ = 8192
DEPTH = 4
DEC_BATCH = 8
DEC_SEQ = 64
PAST_LEN = 1024

CHUNK = 64
N_A_LAYERS = DEPTH // 2
N_B_LAYERS = DEPTH - N_A_LAYERS
EPS = 1e-6
EXPAND = 2
D_INNER = EXPAND * D_MODEL
SSM_HEAD_DIM = 64
SSM_HEADS = D_INNER // SSM_HEAD_DIM
SSM_GROUPS = 4
SSM_STATE = 128
CONV_WIDTH = 4
GN = SSM_GROUPS * SSM_STATE
CONV_DIM = D_INNER + 2 * GN
IN_PROJ_DIM = 2 * D_INNER + 2 * GN + SSM_HEADS
DIFF_HEADS = 8
DIFF_HEAD_DIM = D_MODEL // DIFF_HEADS // 2
DIFF_V_DIM = 2 * DIFF_HEAD_DIM
KV_DIM = DIFF_HEADS * 2 * DIFF_HEAD_DIM + DIFF_HEADS * DIFF_V_DIM
Q_BLOCK = 128
N_GROUPS = 4
EXPERTS_PER_GROUP = 8
N_EXPERTS = N_GROUPS * EXPERTS_PER_GROUP
D_EXPERT = 512
TOP_K = 2

kernel_name = 'yoco_ssd_diffattn_hmoe_stream_step'


def rms_norm(x, gain):
    xf = x.astype(jnp.float32)
    y = xf * lax.rsqrt(jnp.mean(xf * xf, axis=-1, keepdims=True) + EPS)
    return (y * gain.astype(jnp.float32)).astype(x.dtype)


def ada_modulate(x, gain, shift, scale):
    return rms_norm(x, gain) * (1.0 + scale[:, None, :]) + shift[:, None, :]


def alibi_slopes():
    start = 2.0 ** (-8.0 / DIFF_HEADS)
    return jnp.asarray(np.array([start ** (i + 1) for i in range(DIFF_HEADS)], dtype=np.float32))


def causal_conv(xbc, conv_prev, w, b):
    L = xbc.shape[1]
    padded = jnp.concatenate([conv_prev.astype(xbc.dtype), xbc], axis=1)
    out = b
    for k in range(CONV_WIDTH):
        out = out + padded[:, k:k + L] * w[k]
    return jax.nn.silu(out), padded[:, -(CONV_WIDTH - 1):]


def ssd_scan(x, dt, a, b_mat, c_mat, h0, chunk):
    f32 = jnp.float32
    bsz, L, H, P = x.shape
    G, N = SSM_GROUPS, SSM_STATE
    R = H // G
    nc = L // chunk
    xd = (x.astype(f32) * dt[..., None]).reshape(bsz, nc, chunk, G, R, P)
    a_cum = jnp.cumsum((dt * a).reshape(bsz, nc, chunk, G, R), axis=2)
    bm = b_mat.astype(f32).reshape(bsz, nc, chunk, G, N)
    cm = c_mat.astype(f32).reshape(bsz, nc, chunk, G, N)
    tril = jnp.tril(jnp.ones((chunk, chunk), bool))[None, :, :, None, None]

    def step(h, inp):
        xc, ac, bc, cc = inp
        seg = ac[:, :, None] - ac[:, None, :]
        decay = jnp.exp(jnp.where(tril, seg, -jnp.inf))
        cb = jnp.einsum('bign,bjgn->bijg', cc, bc)
        y_diag = jnp.einsum('bijgr,bjgrp->bigrp', cb[..., None] * decay, xc)
        y_off = jnp.einsum('bign,bgrpn->bigrp', cc, h) * jnp.exp(ac)[..., None]
        decay_end = jnp.exp(ac[:, -1:] - ac)
        h_new = jnp.exp(ac[:, -1])[..., None, None] * h + jnp.einsum(
            'bjgn,bjgrp->bgrpn', bc, xc * decay_end[..., None])
        return h_new, y_diag + y_off

    mv = lambda t: jnp.moveaxis(t, 1, 0)
    h_fin, ys = lax.scan(step, h0.astype(f32).reshape(bsz, G, R, P, N),
                         (mv(xd), mv(a_cum), mv(bm), mv(cm)))
    y = jnp.moveaxis(ys, 0, 1).reshape(bsz, L, H, P)
    return y, h_fin.reshape(bsz, H, P, N)


def mamba_mixer(u, conv_prev, h0, chunk, w_in, conv_w, conv_b, dt_bias, a_log, d_skip, norm_g, w_out):
    f32 = jnp.float32
    bsz, L, _ = u.shape
    proj = u @ w_in
    z = proj[..., :D_INNER]
    xbc = proj[..., D_INNER:D_INNER + CONV_DIM]
    dt_raw = proj[..., D_INNER + CONV_DIM:]
    xbc, conv_new = causal_conv(xbc, conv_prev, conv_w, conv_b)
    xs = xbc[..., :D_INNER].reshape(bsz, L, SSM_HEADS, SSM_HEAD_DIM)
    bm = xbc[..., D_INNER:D_INNER + GN].reshape(bsz, L, SSM_GROUPS, SSM_STATE)
    cm = xbc[..., D_INNER + GN:].reshape(bsz, L, SSM_GROUPS, SSM_STATE)
    dt = jax.nn.softplus(dt_raw.astype(f32) + dt_bias.astype(f32))
    a = -jnp.exp(a_log.astype(f32))
    y, h_new = ssd_scan(xs, dt, a, bm, cm, h0, chunk)
    y = y + d_skip.astype(f32)[:, None] * xs.astype(f32)
    y = y.reshape(bsz, L, D_INNER).astype(u.dtype) * jax.nn.silu(z)
    y = rms_norm(y.reshape(bsz, L, SSM_GROUPS, D_INNER // SSM_GROUPS),
                 norm_g.reshape(SSM_GROUPS, D_INNER // SSM_GROUPS)).reshape(bsz, L, D_INNER)
    return y @ w_out, conv_new, h_new.astype(u.dtype)


def diff_attn_core(q1, q2, k1, k2, v, q_pos, k_pos, lam):
    f32 = jnp.float32
    scale = DIFF_HEAD_DIM ** -0.5
    dist = jnp.abs(q_pos[:, None] - k_pos[None, :]).astype(f32)
    visible = (k_pos[None, :] // CHUNK) <= (q_pos[:, None] // CHUNK)
    bias = jnp.where(visible[None], -alibi_slopes()[:, None, None] * dist[None], -jnp.inf)
    s1 = jnp.einsum('bqhd,bkhd->bhqk', q1, k1).astype(f32) * scale + bias
    s2 = jnp.einsum('bqhd,bkhd->bhqk', q2, k2).astype(f32) * scale + bias
    attn = jax.nn.softmax(s1, axis=-1) - lam * jax.nn.softmax(s2, axis=-1)
    return jnp.einsum('bhqk,bkhe->bqhe', attn.astype(v.dtype), v)


def diff_attention_mixer(u, k_all, v_all, pos0, blocked, lam_init, w_q, q_g, lq1, lk1, lq2, lk2, sub_g, w_o):
    f32 = jnp.float32
    bsz, L, _ = u.shape
    q = rms_norm((u @ w_q).reshape(bsz, L, DIFF_HEADS, 2, DIFF_HEAD_DIM), q_g)
    q1, q2 = q[..., 0, :], q[..., 1, :]
    k1, k2 = k_all[..., 0, :], k_all[..., 1, :]
    lam = (jnp.exp(jnp.sum(lq1.astype(f32) * lk1.astype(f32)))
           - jnp.exp(jnp.sum(lq2.astype(f32) * lk2.astype(f32))) + lam_init)
    q_pos = pos0 + jnp.arange(L)
    k_pos = jnp.arange(k_all.shape[1])
    if blocked:
        nb = L // Q_BLOCK
        to_blocks = lambda t: t.reshape(bsz, nb, Q_BLOCK, DIFF_HEADS, DIFF_HEAD_DIM).swapaxes(0, 1)

        def one_block(args):
            q1b, q2b, qpb = args
            return diff_attn_core(q1b, q2b, k1, k2, v_all, qpb, k_pos, lam)

        o = lax.map(one_block, (to_blocks(q1), to_blocks(q2), q_pos.reshape(nb, Q_BLOCK)))
        o = o.swapaxes(0, 1).reshape(bsz, L, DIFF_HEADS, DIFF_V_DIM)
    else:
        o = diff_attn_core(q1, q2, k1, k2, v_all, q_pos, k_pos, lam)
    o = rms_norm(o, sub_g) * (1.0 - lam_init)
    return o.reshape(bsz, L, DIFF_HEADS * DIFF_V_DIM) @ w_o


def shared_kv(h, c, kv_norm_g, kv_ada_w, kv_ada_b, w_kv, k_norm_g):
    bsz, L, _ = h.shape
    mod = (jax.nn.silu(c) @ kv_ada_w + kv_ada_b).reshape(bsz, 2, D_MODEL)
    hn = ada_modulate(h, kv_norm_g, mod[:, 0], mod[:, 1])
    kv = hn @ w_kv
    nk = DIFF_HEADS * 2 * DIFF_HEAD_DIM
    k = rms_norm(kv[..., :nk].reshape(bsz, L, DIFF_HEADS, 2, DIFF_HEAD_DIM), k_norm_g)
    v = kv[..., nk:].reshape(bsz, L, DIFF_HEADS, DIFF_V_DIM)
    return k, v


def hier_moe(u, w_group, b_group, w_expert, b_expert, w1, w3, w2):
    f32 = jnp.float32
    bsz, L, D = u.shape
    t = u.reshape(-1, D)
    T = t.shape[0]
    rows = jnp.arange(T)
    g_logits = (t @ w_group).astype(f32) + b_group.astype(f32)
    g_prob = jax.nn.softmax(g_logits, axis=-1)
    g_idx = jnp.argmax(g_logits, axis=-1)
    g_w = g_prob[rows, g_idx]
    e_logits = ((t @ w_expert).astype(f32) + b_expert.astype(f32)).reshape(T, N_GROUPS, EXPERTS_PER_GROUP)
    e_prob = jax.nn.softmax(e_logits[rows, g_idx], axis=-1)
    top_p, top_i = lax.top_k(e_prob, TOP_K)
    top_w = top_p / jnp.sum(top_p, axis=-1, keepdims=True) * g_w[:, None]
    e_idx = g_idx[:, None] * EXPERTS_PER_GROUP + top_i
    comb = jnp.sum(jax.nn.one_hot(e_idx, N_EXPERTS, dtype=f32) * top_w[..., None], axis=1)
    out = jnp.zeros_like(t)
    for e in range(N_EXPERTS):
        hid = jax.nn.silu(t @ w1[e]) * (t @ w3[e])
        out = out + comb[:, e:e + 1].astype(t.dtype) * (hid @ w2[e])
    return out.reshape(bsz, L, D)


def trunk(x, c, conv_prev, ssm_prev, k_past, v_past, ssd_chunk, layer_w, ssm_w, kv_w, attn_w, moe_w):
    ada_w, ada_b, norm_g = layer_w
    s_w_in, s_conv_w, s_conv_b, s_dt_bias, s_a_log, s_d, s_norm_g, s_w_out = ssm_w
    kv_norm_g, kv_ada_w, kv_ada_b, w_kv, k_norm_g = kv_w
    a_w_q, a_q_g, a_lq1, a_lk1, a_lq2, a_lk2, a_sub_g, a_w_o = attn_w
    m_wg, m_bg, m_we, m_be, m_w1, m_w3, m_w2 = moe_w
    pos0 = 0 if k_past is None else k_past.shape[1]
    h = x
    conv_new, ssm_new = [], []
    k_new = v_new = k_all = v_all = None
    for layer in range(DEPTH):
        mod = (jax.nn.silu(c) @ ada_w[layer] + ada_b[layer]).reshape(-1, 6, D_MODEL)
        u = ada_modulate(h, norm_g[layer, 0], mod[:, 0], mod[:, 1])
        if layer < N_A_LAYERS:
            i = layer
            mix, cn, sn = mamba_mixer(u, conv_prev[i], ssm_prev[i], ssd_chunk, s_w_in[i], s_conv_w[i],
                                      s_conv_b[i], s_dt_bias[i], s_a_log[i], s_d[i], s_norm_g[i], s_w_out[i])
            conv_new.append(cn)
            ssm_new.append(sn)
        else:
            j = layer - N_A_LAYERS
            lam_init = 0.8 - 0.6 * math.exp(-0.3 * layer)
            mix = diff_attention_mixer(u, k_all, v_all, pos0, k_past is None, lam_init, a_w_q[j], a_q_g[j],
                                       a_lq1[j], a_lk1[j], a_lq2[j], a_lk2[j], a_sub_g[j], a_w_o[j])
        h = h + mod[:, 2, None, :] * mix
        u = ada_modulate(h, norm_g[layer, 1], mod[:, 3], mod[:, 4])
        h = h + mod[:, 5, None, :] * hier_moe(u, m_wg[layer], m_bg[layer], m_we[layer], m_be[layer],
                                             m_w1[layer], m_w3[layer], m_w2[layer])
        if layer == N_A_LAYERS - 1:
            k_new, v_new = shared_kv(h, c, kv_norm_g, kv_ada_w, kv_ada_b, w_kv, k_norm_g)
            if k_past is None:
                k_all, v_all = k_new, v_new
            else:
                k_all = jnp.concatenate([k_past.astype(k_new.dtype), k_new], axis=1)
                v_all = jnp.concatenate([v_past.astype(v_new.dtype), v_new], axis=1)
    return h, jnp.stack(conv_new), jnp.stack(ssm_new), k_new, v_new


def setup_inputs(seed: int = 0) -> dict:
    key = jax.random.key(seed)
    ks = iter(jax.random.split(key, 64))
    f32 = jnp.float32

    def nrm(shape, scale=1.0):
        return jax.random.normal(next(ks), shape, f32) * scale

    dt0 = jnp.exp(jax.random.uniform(next(ks), (N_A_LAYERS, SSM_HEADS), f32)
                  * (math.log(0.1) - math.log(0.001)) + math.log(0.001))
    dt_bias = dt0 + jnp.log(-jnp.expm1(-dt0))
    a_log = jnp.log(jax.random.uniform(next(ks), (N_A_LAYERS, SSM_HEADS), f32, 1.0, 16.0))
    hd = DIFF_HEAD_DIM
    return {
        'x_prompt': nrm((BATCH, SEQ, D_MODEL)),
        'x_sample': nrm((DEC_BATCH, DEC_SEQ, D_MODEL)),
        'c_prompt': nrm((BATCH, D_MODEL)),
        'c_sample': nrm((DEC_BATCH, D_MODEL)),
        'state_conv': nrm((N_A_LAYERS, DEC_BATCH, CONV_WIDTH - 1, CONV_DIM)),
        'state_ssm': nrm((N_A_LAYERS, DEC_BATCH, SSM_HEADS, SSM_HEAD_DIM, SSM_STATE), 0.5),
        'cache_k': nrm((DEC_BATCH, PAST_LEN, DIFF_HEADS, 2, hd)),
        'cache_v': nrm((DEC_BATCH, PAST_LEN, DIFF_HEADS, DIFF_V_DIM)),
        'ada_w': nrm((DEPTH, D_MODEL, 6 * D_MODEL), 0.5 * D_MODEL ** -0.5),
        'ada_b': nrm((DEPTH, 6 * D_MODEL), 0.01),
        'norm_g': 1.0 + nrm((DEPTH, 2, D_MODEL), 0.01),
        'ssm_w_in': nrm((N_A_LAYERS, D_MODEL, IN_PROJ_DIM), D_MODEL ** -0.5),
        'ssm_conv_w': nrm((N_A_LAYERS, CONV_WIDTH, CONV_DIM), CONV_WIDTH ** -0.5),
        'ssm_conv_b': nrm((N_A_LAYERS, CONV_DIM), 0.01),
        'ssm_dt_bias': dt_bias,
        'ssm_a_log': a_log,
        'ssm_d': 1.0 + nrm((N_A_LAYERS, SSM_HEADS), 0.01),
        'ssm_norm_g': 1.0 + nrm((N_A_LAYERS, D_INNER), 0.01),
        'ssm_w_out': nrm((N_A_LAYERS, D_INNER, D_MODEL), D_INNER ** -0.5),
        'kv_norm_g': 1.0 + nrm((D_MODEL,), 0.01),
        'kv_ada_w': nrm((D_MODEL, 2 * D_MODEL), 0.5 * D_MODEL ** -0.5),
        'kv_ada_b': nrm((2 * D_MODEL,), 0.01),
        'w_kv': nrm((D_MODEL, KV_DIM), D_MODEL ** -0.5),
        'k_norm_g': 1.0 + nrm((hd,), 0.01),
        'attn_w_q': nrm((N_B_LAYERS, D_MODEL, DIFF_HEADS * 2 * hd), D_MODEL ** -0.5),
        'attn_q_norm_g': 1.0 + nrm((N_B_LAYERS, hd), 0.01),
        'attn_lambda_q1': nrm((N_B_LAYERS, hd), 0.1),
        'attn_lambda_k1': nrm((N_B_LAYERS, hd), 0.1),
        'attn_lambda_q2': nrm((N_B_LAYERS, hd), 0.1),
        'attn_lambda_k2': nrm((N_B_LAYERS, hd), 0.1),
        'attn_sub_g': 1.0 + nrm((N_B_LAYERS, DIFF_V_DIM), 0.01),
        'attn_w_o': nrm((N_B_LAYERS, DIFF_HEADS * DIFF_V_DIM, D_MODEL), (DIFF_HEADS * DIFF_V_DIM) ** -0.5),
        'moe_w_group': nrm((DEPTH, D_MODEL, N_GROUPS), D_MODEL ** -0.5),
        'moe_b_group': nrm((DEPTH, N_GROUPS), 0.01),
        'moe_w_expert': nrm((DEPTH, D_MODEL, N_EXPERTS), D_MODEL ** -0.5),
        'moe_b_expert': nrm((DEPTH, N_EXPERTS), 0.01),
        'moe_w1': nrm((DEPTH, N_EXPERTS, D_MODEL, D_EXPERT), D_MODEL ** -0.5),
        'moe_w3': nrm((DEPTH, N_EXPERTS, D_MODEL, D_EXPERT), D_MODEL ** -0.5),
        'moe_w2': nrm((DEPTH, N_EXPERTS, D_EXPERT, D_MODEL), D_EXPERT ** -0.5),
    }


def reference(x_prompt, x_sample, c_prompt, c_sample, state_conv, state_ssm, cache_k, cache_v,
              ada_w, ada_b, norm_g, ssm_w_in, ssm_conv_w, ssm_conv_b, ssm_dt_bias, ssm_a_log, ssm_d,
              ssm_norm_g, ssm_w_out, kv_norm_g, kv_ada_w, kv_ada_b, w_kv, k_norm_g, attn_w_q,
              attn_q_norm_g, attn_lambda_q1, attn_lambda_k1, attn_lambda_q2, attn_lambda_k2, attn_sub_g,
              attn_w_o, moe_w_group, moe_b_group, moe_w_expert, moe_b_expert, moe_w1, moe_w3, moe_w2):
    layer_w = (ada_w, ada_b, norm_g)
    ssm_w = (ssm_w_in, ssm_conv_w, ssm_conv_b, ssm_dt_bias, ssm_a_log, ssm_d, ssm_norm_g, ssm_w_out)
    kv_w = (kv_norm_g, kv_ada_w, kv_ada_b, w_kv, k_norm_g)
    attn_w = (attn_w_q, attn_q_norm_g, attn_lambda_q1, attn_lambda_k1, attn_lambda_q2, attn_lambda_k2,
              attn_sub_g, attn_w_o)
    moe_w = (moe_w_group, moe_b_group, moe_w_expert, moe_b_expert, moe_w1, moe_w3, moe_w2)
    bp = x_prompt.shape[0]
    conv0 = jnp.zeros((N_A_LAYERS, bp, CONV_WIDTH - 1, CONV_DIM), x_prompt.dtype)
    ssm0 = jnp.zeros((N_A_LAYERS, bp, SSM_HEADS, SSM_HEAD_DIM, SSM_STATE), x_prompt.dtype)
    y_prompt, conv_p, ssm_p, k_p, v_p = trunk(x_prompt, c_prompt, conv0, ssm0, None, None, CHUNK,
                                              layer_w, ssm_w, kv_w, attn_w, moe_w)
    y_sample, conv_s, ssm_s, k_s, v_s = trunk(x_sample, c_sample, state_conv, state_ssm, cache_k, cache_v,
                                              x_sample.shape[1], layer_w, ssm_w, kv_w, attn_w, moe_w)
    return (y_prompt, y_sample, conv_p, ssm_p, k_p, v_p, conv_s, ssm_s, k_s, v_s)
```

```python
import functools
import math

import jax
import jax.numpy as jnp
from jax import lax
from jax.experimental import pallas as pl
from jax.experimental.pallas import tpu as pltpu

F32 = jnp.float32
BF16 = jnp.bfloat16
HIGHEST = lax.Precision.HIGHEST

D_MODEL = 1024
DEPTH = 4
N_A_LAYERS = 2
EPS = 1e-6
CHUNK = 64
D_INNER = 2048
SSM_HEADS = 32
SSM_HEAD_DIM = 64
SSM_GROUPS = 4
SSM_STATE = 128
GROUP_WIDTH = D_INNER // SSM_GROUPS
CONV_WIDTH = 4
CONV_DIM = D_INNER + 2 * SSM_GROUPS * SSM_STATE
DT_PAD = 128
IN_PROJ_PAD = D_INNER + CONV_DIM + DT_PAD
DIFF_HEADS = 8
DIFF_HEAD_DIM = 64
DIFF_V_DIM = 128
N_GROUPS = 4
EXPERTS_PER_GROUP = 8
N_EXPERTS = 32
D_EXPERT = 512

LANES = 128
NEG_BIG = -1e30
VMEM_LIMIT = 56 * 1024 * 1024


def _params(semantics, vmem=VMEM_LIMIT):
    return pltpu.CompilerParams(dimension_semantics=semantics, vmem_limit_bytes=vmem)


def _silu(x):
    return x * (1.0 / (1.0 + jnp.exp(-x)))


def _softplus(x):
    return jnp.maximum(x, 0.0) + jnp.log1p(jnp.exp(-jnp.abs(x)))


def _mod_norm(h, gain, shift, scale):
    ms = jnp.mean(h * h, axis=-1, keepdims=True)
    return (h * lax.rsqrt(ms + EPS) * gain) * (1.0 + scale) + shift


def _seg_rms64(x, g, n_cols):
    lane = lax.broadcasted_iota(jnp.int32, (x.shape[0], LANES), 1)
    low = lane < 64
    outs = []
    for cb in range(n_cols // LANES):
        xc = x[:, cb * LANES:(cb + 1) * LANES]
        sq = xc * xc
        s_lo = jnp.sum(jnp.where(low, sq, 0.0), axis=-1, keepdims=True)
        s_hi = jnp.sum(jnp.where(low, 0.0, sq), axis=-1, keepdims=True)
        r = jnp.where(low, lax.rsqrt(s_lo * (1.0 / 64) + EPS), lax.rsqrt(s_hi * (1.0 / 64) + EPS))
        outs.append(xc * r * g[:, cb * LANES:(cb + 1) * LANES])
    if n_cols < x.shape[1]:
        outs.append(x[:, n_cols:])
    return jnp.concatenate(outs, axis=1)


def _mods_kernel(c_ref, w_ref, b_ref, o_ref):
    s = _silu(c_ref[...])
    o_ref[0] = jnp.dot(s.astype(BF16), w_ref[0].astype(BF16), preferred_element_type=F32) + b_ref[0]


def _mods(c_all, w, b, tn):
    nl, _, n = w.shape
    m = c_all.shape[0]
    return pl.pallas_call(
        _mods_kernel,
        grid=(nl, n // tn),
        in_specs=[
            pl.BlockSpec((m, D_MODEL), lambda l, j: (0, 0)),
            pl.BlockSpec((1, D_MODEL, tn), lambda l, j: (l, 0, j)),
            pl.BlockSpec((1, 1, tn), lambda l, j: (l, 0, j)),
        ],
        out_specs=pl.BlockSpec((1, m, tn), lambda l, j: (l, 0, j)),
        out_shape=jax.ShapeDtypeStruct((nl, m, n), F32),
        compiler_params=_params(("parallel", "parallel")),
        name="adaln_mods",
    )(c_all, w, b)


def _expand_heads(v, e2_ref):
    hi = v.astype(BF16)
    lo = (v - hi.astype(F32)).astype(BF16)
    return jnp.dot(jnp.concatenate([hi, lo], axis=1), e2_ref[...], preferred_element_type=F32)


def _mamba_kernel(h_ref, mod_ref, ng_ref, w_ref, cw_ref, cb_ref, dtb_ref, alog_ref, dsk_ref, sng_ref,
                  wo_ref, cprev_ref, sprev_ref, e2_ref,
                  hout_ref, cnew_ref, snew_ref,
                  xbc_scr, st_scr, xd_scr, y_scr, *, lt, lc, nc):
    c = pl.program_id(1)

    @pl.when(c == 0)
    def _():
        xbc_scr[0:8, :] = cprev_ref[0]
        st_scr[...] = sprev_ref[0].T

    h = h_ref[0]
    mod = mod_ref[0]
    u = _mod_norm(h, ng_ref[...], mod[0:1], mod[1:2])
    proj = jnp.dot(u.astype(BF16), w_ref[...], preferred_element_type=F32)
    z = proj[:, :D_INNER]
    xbc_scr[8:8 + lt, :] = proj[:, D_INNER:D_INNER + CONV_DIM]
    dt_raw = proj[:, D_INNER + CONV_DIM:]

    cw = cw_ref[...]
    conv = cb_ref[...]
    for k in range(CONV_WIDTH):
        conv = conv + cw[k:k + 1] * xbc_scr[5 + k:5 + k + lt, :]
    tail = xbc_scr[lt:lt + 8, :]
    cnew_ref[0] = tail
    xbc_scr[0:8, :] = tail
    xbc = _silu(conv)
    xs = xbc[:, :D_INNER]
    bm = xbc[:, D_INNER:D_INNER + GROUP_WIDTH]
    cm = xbc[:, D_INNER + GROUP_WIDTH:].astype(BF16)

    dt = _softplus(dt_raw + dtb_ref[...])
    da = dt * (-jnp.exp(alog_ref[...]))
    xd_scr[...] = xs * _expand_heads(dt, e2_ref)

    row_i = lax.broadcasted_iota(jnp.int32, (lc, lc), 0)
    col_j = lax.broadcasted_iota(jnp.int32, (lc, lc), 1)
    tril = col_j <= row_i
    trilf = tril.astype(F32)
    low = lax.broadcasted_iota(jnp.int32, (lc, LANES), 1) < SSM_HEAD_DIM

    for s in range(lt // lc):
        r0 = s * lc
        acum = jnp.dot(trilf, da[r0:r0 + lc], precision=HIGHEST, preferred_element_type=F32)
        acum_t = acum.T
        acum_e = _expand_heads(acum, e2_ref)
        ea = jnp.exp(acum_e)
        dend = jnp.exp(acum_e[lc - 1:lc] - acum_e)
        xd_s = xd_scr[r0:r0 + lc, :]
        xdb = xd_s.astype(BF16)
        xdw = (xd_s * dend).astype(BF16)
        for g in range(SSM_GROUPS):
            gl = slice(g * GROUP_WIDTH, (g + 1) * GROUP_WIDTH)
            b_g = bm[r0:r0 + lc, g * SSM_STATE:(g + 1) * SSM_STATE]
            c_g = cm[r0:r0 + lc, g * SSM_STATE:(g + 1) * SSM_STATE]
            cbm = lax.dot_general(c_g, b_g.astype(BF16), (((1,), (1,)), ((), ())),
                                  preferred_element_type=F32)
            s_g = st_scr[:, gl]
            yoff = jnp.dot(c_g, s_g.astype(BF16), preferred_element_type=F32) * ea[:, gl]
            for j in range(GROUP_WIDTH // LANES):
                pair = g * (GROUP_WIDTH // LANES) + j
                pl_ = slice(pair * LANES, (pair + 1) * LANES)
                ys = []
                for hh in (2 * pair, 2 * pair + 1):
                    seg = acum[:, hh:hh + 1] - acum_t[hh:hh + 1, :]
                    dec = jnp.exp(jnp.where(tril, seg, NEG_BIG))
                    ys.append(jnp.dot((cbm * dec).astype(BF16), xdb[:, pl_], preferred_element_type=F32))
                y_scr[r0:r0 + lc, pl_] = jnp.where(low, ys[0], ys[1]) + yoff[:, j * LANES:(j + 1) * LANES]
            upd = jnp.dot(b_g.T.astype(BF16), xdw[:, gl], preferred_element_type=F32)
            st_scr[:, gl] = s_g * ea[lc - 1:lc, gl] + upd

    y = (y_scr[...] + dsk_ref[...] * xs) * _silu(z)
    parts = []
    for g in range(SSM_GROUPS):
        gl = slice(g * GROUP_WIDTH, (g + 1) * GROUP_WIDTH)
        yg = y[:, gl]
        ms = jnp.mean(yg * yg, axis=-1, keepdims=True)
        parts.append(yg * lax.rsqrt(ms + EPS) * sng_ref[:, gl])
    yn = jnp.concatenate(parts, axis=1).astype(BF16)
    out = jnp.dot(yn, wo_ref[...], preferred_element_type=F32)
    hout_ref[0] = h + mod[2:3] * out

    @pl.when(c == nc - 1)
    def _():
        snew_ref[0] = st_scr[...].T


def _mamba_layer(h, mod, ng, w_in, cw, cb, dtb, alog, dsk, sng, wo, cprev, sprev, e2, lt, lc):
    bsz, seq, _ = h.shape
    nc = seq // lt
    const2 = lambda b, c: (0, 0)
    perb = lambda b, c: (b, 0, 0)
    kern = functools.partial(_mamba_kernel, lt=lt, lc=lc, nc=nc)
    return pl.pallas_call(
        kern,
        grid=(bsz, nc),
        in_specs=[
            pl.BlockSpec((1, lt, D_MODEL), lambda b, c: (b, c, 0)),
            pl.BlockSpec((1, 6, D_MODEL), perb),
            pl.BlockSpec((1, D_MODEL), const2),
            pl.BlockSpec((D_MODEL, IN_PROJ_PAD), const2),
            pl.BlockSpec((CONV_WIDTH, CONV_DIM), const2),
            pl.BlockSpec((1, CONV_DIM), const2),
            pl.BlockSpec((1, DT_PAD), const2),
            pl.BlockSpec((1, DT_PAD), const2),
            pl.BlockSpec((1, D_INNER), const2),
            pl.BlockSpec((1, D_INNER), const2),
            pl.BlockSpec((D_INNER, D_MODEL), const2),
            pl.BlockSpec((1, 8, CONV_DIM), perb),
            pl.BlockSpec((1, D_INNER, SSM_STATE), perb),
            pl.BlockSpec((2 * DT_PAD, D_INNER), const2),
        ],
        out_specs=[
            pl.BlockSpec((1, lt, D_MODEL), lambda b, c: (b, c, 0)),
            pl.BlockSpec((1, 8, CONV_DIM), perb),
            pl.BlockSpec((1, D_INNER, SSM_STATE), perb),
        ],
        out_shape=[
            jax.ShapeDtypeStruct((bsz, seq, D_MODEL), F32),
            jax.ShapeDtypeStruct((bsz, 8, CONV_DIM), F32),
            jax.ShapeDtypeStruct((bsz, D_INNER, SSM_STATE), F32),
        ],
        scratch_shapes=[
            pltpu.VMEM((lt + 8, CONV_DIM), F32),
            pltpu.VMEM((SSM_STATE, D_INNER), F32),
            pltpu.VMEM((lt, D_INNER), F32),
            pltpu.VMEM((lt, D_INNER), F32),
        ],
        compiler_params=_params(("parallel", "arbitrary")),
        name="mamba_layer",
    )(h, mod, ng, w_in, cw, cb, dtb, alog, dsk, sng, wo, cprev, sprev, e2)


def _normproj_kernel(h_ref, mod_ref, ng_ref, w_ref, sg_ref, *out_refs, n_norm, want_f32):
    mod = mod_ref[0]
    u = _mod_norm(h_ref[0], ng_ref[...], mod[0:1], mod[1:2])
    y = jnp.dot(u.astype(BF16), w_ref[...], preferred_element_type=F32)
    y = _seg_rms64(y, sg_ref[...], n_norm)
    if want_f32:
        out_refs[0][0] = y
    out_refs[-1][0] = y.astype(BF16)


def _normproj(h, mod2, ng, w, seg_g, tl, want_f32, name):
    bsz, seq, _ = h.shape
    nout = w.shape[1]
    n_norm = seg_g.shape[1]
    tok = lambda b, t: (b, t, 0)
    const2 = lambda b, t: (0, 0)
    out_specs = [pl.BlockSpec((1, tl, nout), tok)]
    out_shape = [jax.ShapeDtypeStruct((bsz, seq, nout), BF16)]
    if want_f32:
        out_specs = [pl.BlockSpec((1, tl, nout), tok)] + out_specs
        out_shape = [jax.ShapeDtypeStruct((bsz, seq, nout), F32)] + out_shape
    return pl.pallas_call(
        functools.partial(_normproj_kernel, n_norm=n_norm, want_f32=want_f32),
        grid=(bsz, seq // tl),
        in_specs=[
            pl.BlockSpec((1, tl, D_MODEL), tok),
            pl.BlockSpec((1, 2, D_MODEL), lambda b, t: (b, 0, 0)),
            pl.BlockSpec((1, D_MODEL), const2),
            pl.BlockSpec((D_MODEL, nout), const2),
            pl.BlockSpec((1, n_norm), const2),
        ],
        out_specs=out_specs,
        out_shape=out_shape,
        compiler_params=_params(("parallel", "parallel")),
        name=name,
    )(h, mod2, ng, w, seg_g)


def _attn_kernel(slope_ref, q_ref, k_ref, v_ref, lam_ref, sg_ref, o_ref,
                 m1, l1, a1, m2, l2, a2, *, tq, tk, nk, pos0, lam_init):
    hd = pl.program_id(1)
    qi = pl.program_id(2)
    ki = pl.program_id(3)

    @pl.when(ki == 0)
    def _():
        for m, l, a in ((m1, l1, a1), (m2, l2, a2)):
            m[...] = jnp.full(m.shape, NEG_BIG, F32)
            l[...] = jnp.zeros(l.shape, F32)
            a[...] = jnp.zeros(a.shape, F32)

    q_first = pos0 + qi * tq
    last_q_chunk = (q_first + tq - 1) // CHUNK
    first_k_chunk = (ki * tk) // CHUNK

    @pl.when(first_k_chunk <= last_q_chunk)
    def _():
        q = q_ref[0]
        k = k_ref[0]
        v = v_ref[0]
        lane = lax.broadcasted_iota(jnp.int32, q.shape, 1)
        zero = jnp.zeros_like(q)
        qa = jnp.where(lane < DIFF_HEAD_DIM, q, zero)
        qb = jnp.where(lane < DIFF_HEAD_DIM, zero, q)
        qpos = q_first + lax.broadcasted_iota(jnp.int32, (tq, tk), 0)
        kpos = ki * tk + lax.broadcasted_iota(jnp.int32, (tq, tk), 1)
        visible = (kpos // CHUNK) <= (qpos // CHUNK)
        dist = jnp.abs(qpos - kpos).astype(F32)
        bias = jnp.where(visible, -slope_ref[hd] * dist, NEG_BIG)
        scale = DIFF_HEAD_DIM ** -0.5
        nt = (((1,), (1,)), ((), ()))
        for qq, m, l, a in ((qa, m1, l1, a1), (qb, m2, l2, a2)):
            s = lax.dot_general(qq, k, nt, preferred_element_type=F32) * scale + bias
            m_old = m[...]
            m_new = jnp.maximum(m_old, jnp.max(s, axis=-1, keepdims=True))
            p = jnp.exp(s - m_new)
            alpha = jnp.exp(m_old - m_new)
            l[...] = alpha * l[...] + jnp.sum(p, axis=-1, keepdims=True)
            a[...] = alpha * a[...] + jnp.dot(p.astype(BF16), v, preferred_element_type=F32)
            m[...] = m_new

    @pl.when(ki == nk - 1)
    def _():
        lp = lam_ref[...]
        lam = (jnp.exp(jnp.sum(lp[0:1] * lp[1:2], axis=-1, keepdims=True))
               - jnp.exp(jnp.sum(lp[2:3] * lp[3:4], axis=-1, keepdims=True)) + lam_init)
        o = a1[...] / l1[...] - lam * (a2[...] / l2[...])
        ms = jnp.mean(o * o, axis=-1, keepdims=True)
        o_ref[0] = (o * lax.rsqrt(ms + EPS) * sg_ref[...] * (1.0 - lam_init)).astype(BF16)


def _diff_attention(q, k, v, slopes, lam_rows, sub_g, tq, tk, pos0, lam_init):
    bsz, lq, _ = q.shape
    lk = k.shape[1]
    nq, nk = lq // tq, lk // tk

    def kv_map(b, h, i, j, slope_ref):
        last_visible = ((pos0 + i * tq + tq - 1) // CHUNK * CHUNK + CHUNK - 1) // tk
        return (b, jnp.minimum(j, last_visible), h)

    grid_spec = pltpu.PrefetchScalarGridSpec(
        num_scalar_prefetch=1,
        grid=(bsz, DIFF_HEADS, nq, nk),
        in_specs=[
            pl.BlockSpec((1, tq, LANES), lambda b, h, i, j, s: (b, i, h)),
            pl.BlockSpec((1, tk, LANES), kv_map),
            pl.BlockSpec((1, tk, LANES), kv_map),
            pl.BlockSpec((8, LANES), lambda b, h, i, j, s: (0, 0)),
            pl.BlockSpec((1, LANES), lambda b, h, i, j, s: (0, 0)),
        ],
        out_specs=pl.BlockSpec((1, tq, LANES), lambda b, h, i, j, s: (b, i, h)),
        scratch_shapes=[
            pltpu.VMEM((tq, 1), F32), pltpu.VMEM((tq, 1), F32), pltpu.VMEM((tq, LANES), F32),
            pltpu.VMEM((tq, 1), F32), pltpu.VMEM((tq, 1), F32), pltpu.VMEM((tq, LANES), F32),
        ],
    )
    return pl.pallas_call(
        functools.partial(_attn_kernel, tq=tq, tk=tk, nk=nk, pos0=pos0, lam_init=lam_init),
        grid_spec=grid_spec,
        out_shape=jax.ShapeDtypeStruct((bsz, lq, DIFF_HEADS * DIFF_V_DIM), BF16),
        compiler_params=_params(("parallel", "parallel", "parallel", "arbitrary")),
        name="diff_attention",
    )(slopes, q, k, v, lam_rows, sub_g)


def _proj_res_kernel(x_ref, w_ref, h_ref, gate_ref, o_ref):
    y = jnp.dot(x_ref[0], w_ref[...], preferred_element_type=F32)
    o_ref[0] = h_ref[0] + gate_ref[0] * y


def _proj_res(x, w, h, gate, tl):
    bsz, seq, kdim = x.shape
    tok = lambda b, t: (b, t, 0)
    return pl.pallas_call(
        _proj_res_kernel,
        grid=(bsz, seq // tl),
        in_specs=[
            pl.BlockSpec((1, tl, kdim), tok),
            pl.BlockSpec((kdim, D_MODEL), lambda b, t: (0, 0)),
            pl.BlockSpec((1, tl, D_MODEL), tok),
            pl.BlockSpec((1, 1, D_MODEL), lambda b, t: (b, 0, 0)),
        ],
        out_specs=pl.BlockSpec((1, tl, D_MODEL), tok),
        out_shape=jax.ShapeDtypeStruct((bsz, seq, D_MODEL), F32),
        compiler_params=_params(("parallel", "parallel")),
        name="attn_out_proj",
    )(x, w, h, gate)


def _moe_route_kernel(h_ref, mod_ref, ng_ref, wr_ref, br_ref, u_ref, comb_ref):
    mod = mod_ref[0]
    u = _mod_norm(h_ref[0], ng_ref[...], mod[0:1], mod[1:2])
    ub = u.astype(BF16)
    u_ref[0] = ub
    logits = jnp.dot(ub, wr_ref[...], preferred_element_type=F32) + br_ref[...]
    gl = logits[:, :LANES]
    el = logits[:, LANES:]
    lane = lax.broadcasted_iota(jnp.int32, gl.shape, 1)
    lane_f = lane.astype(F32)
    none = float(LANES)

    gmask = lane < N_GROUPS
    glm = jnp.where(gmask, gl, NEG_BIG)
    gmax = jnp.max(glm, axis=-1, keepdims=True)
    g_idx = jnp.min(jnp.where(glm == gmax, lane_f, none), axis=-1, keepdims=True)
    g_w = 1.0 / jnp.sum(jnp.where(gmask, jnp.exp(glm - gmax), 0.0), axis=-1, keepdims=True)

    lo = g_idx * EXPERTS_PER_GROUP
    emask = (lane_f >= lo) & (lane_f < lo + EXPERTS_PER_GROUP)
    elm = jnp.where(emask, el, NEG_BIG)
    emax = jnp.max(elm, axis=-1, keepdims=True)
    ee = jnp.where(emask, jnp.exp(elm - emax), 0.0)
    ep = ee / jnp.sum(ee, axis=-1, keepdims=True)
    epm = jnp.where(emask, ep, -1.0)
    p1 = jnp.max(epm, axis=-1, keepdims=True)
    i1 = jnp.min(jnp.where(epm == p1, lane_f, none), axis=-1, keepdims=True)
    epm2 = jnp.where(lane_f == i1, -1.0, epm)
    p2 = jnp.max(epm2, axis=-1, keepdims=True)
    i2 = jnp.min(jnp.where(epm2 == p2, lane_f, none), axis=-1, keepdims=True)
    denom = p1 + p2
    comb_ref[0] = (jnp.where(lane_f == i1, p1 / denom * g_w, 0.0)
                   + jnp.where(lane_f == i2, p2 / denom * g_w, 0.0))


def _moe_route(h, mod2, ng, wr, br, tl):
    bsz, seq, _ = h.shape
    tok = lambda b, t: (b, t, 0)
    const2 = lambda b, t: (0, 0)
    return pl.pallas_call(
        _moe_route_kernel,
        grid=(bsz, seq // tl),
        in_specs=[
            pl.BlockSpec((1, tl, D_MODEL), tok),
            pl.BlockSpec((1, 2, D_MODEL), lambda b, t: (b, 0, 0)),
            pl.BlockSpec((1, D_MODEL), const2),
            pl.BlockSpec((D_MODEL, 2 * LANES), const2),
            pl.BlockSpec((1, 2 * LANES), const2),
        ],
        out_specs=[pl.BlockSpec((1, tl, D_MODEL), tok), pl.BlockSpec((1, tl, LANES), tok)],
        out_shape=[jax.ShapeDtypeStruct((bsz, seq, D_MODEL), BF16),
                   jax.ShapeDtypeStruct((bsz, seq, LANES), F32)],
        compiler_params=_params(("parallel", "parallel")),
        name="moe_route",
    )(h, mod2, ng, wr, br)


def _moe_dense_kernel(u_ref, comb_ref, w1_ref, w3_ref, w2_ref, h_ref, gate_ref, o_ref, acc):
    e = pl.program_id(2)

    @pl.when(e == 0)
    def _():
        acc[...] = jnp.zeros(acc.shape, F32)

    x = u_ref[0]
    hid = _silu(jnp.dot(x, w1_ref[0], preferred_element_type=F32)) * jnp.dot(
        x, w3_ref[0], preferred_element_type=F32)
    y = jnp.dot(hid.astype(BF16), w2_ref[0], preferred_element_type=F32)
    comb = comb_ref[0]
    lane = lax.broadcasted_iota(jnp.int32, comb.shape, 1)
    col = jnp.sum(jnp.where(lane == e, comb, 0.0), axis=-1, keepdims=True)
    acc[...] += col * y

    @pl.when(e == N_EXPERTS - 1)
    def _():
        o_ref[0] = h_ref[0] + gate_ref[0] * acc[...]


def _moe_dense(u, comb, w1, w3, w2, h, gate, tl):
    bsz, seq, _ = u.shape
    tok = lambda b, t, e: (b, t, 0)
    return pl.pallas_call(
        _moe_dense_kernel,
        grid=(bsz, seq // tl, N_EXPERTS),
        in_specs=[
            pl.BlockSpec((1, tl, D_MODEL), tok),
            pl.BlockSpec((1, tl, LANES), tok),
            pl.BlockSpec((1, D_MODEL, D_EXPERT), lambda b, t, e: (e, 0, 0)),
            pl.BlockSpec((1, D_MODEL, D_EXPERT), lambda b, t, e: (e, 0, 0)),
            pl.BlockSpec((1, D_EXPERT, D_MODEL), lambda b, t, e: (e, 0, 0)),
            pl.BlockSpec((1, tl, D_MODEL), tok),
            pl.BlockSpec((1, 1, D_MODEL), lambda b, t, e: (b, 0, 0)),
        ],
        out_specs=pl.BlockSpec((1, tl, D_MODEL), tok),
        out_shape=jax.ShapeDtypeStruct((bsz, seq, D_MODEL), F32),
        scratch_shapes=[pltpu.VMEM((tl, D_MODEL), F32)],
        compiler_params=_params(("parallel", "parallel", "arbitrary")),
        name="moe_dense",
    )(u, comb, w1, w3, w2, h, gate)


def _alibi_slopes():
    start = 2.0 ** (-8.0 / DIFF_HEADS)
    return jnp.asarray([start ** (i + 1) for i in range(DIFF_HEADS)], F32)


def _head_expand_matrix():
    head_of_lane = jnp.arange(D_INNER) // SSM_HEAD_DIM
    e = (jnp.arange(DT_PAD)[:, None] == head_of_lane[None, :]).astype(BF16)
    return jnp.concatenate([e, e], axis=0)


def _pad_lanes(x, width):
    return jnp.pad(x, [(0, 0)] * (x.ndim - 1) + [(0, width - x.shape[-1])])


def _prep_weights(p):
    w = {}
    w_in = p["ssm_w_in"]
    w["ssm_w_in"] = jnp.concatenate(
        [w_in[..., :D_INNER + CONV_DIM], _pad_lanes(w_in[..., D_INNER + CONV_DIM:], DT_PAD)], axis=-1).astype(BF16)
    w["ssm_dt_bias"] = _pad_lanes(p["ssm_dt_bias"], DT_PAD)[:, None, :]
    w["ssm_a_log"] = _pad_lanes(p["ssm_a_log"], DT_PAD)[:, None, :]
    w["ssm_d"] = jnp.repeat(p["ssm_d"], SSM_HEAD_DIM, axis=-1)[:, None, :]
    w["ssm_w_out"] = p["ssm_w_out"].astype(BF16)
    w["e2"] = _head_expand_matrix()
    w["w_kv"] = p["w_kv"].astype(BF16)
    w["k_norm_g"] = jnp.tile(p["k_norm_g"], 2 * DIFF_HEADS)[None, :]
    w["attn_w_q"] = p["attn_w_q"].astype(BF16)
    w["attn_q_norm_g"] = jnp.tile(p["attn_q_norm_g"], (1, 2 * DIFF_HEADS))[:, None, :]
    lam = jnp.stack([p["attn_lambda_q1"], p["attn_lambda_k1"], p["attn_lambda_q2"], p["attn_lambda_k2"]], axis=1)
    w["lam_rows"] = jnp.pad(lam, ((0, 0), (0, 4), (0, LANES - DIFF_HEAD_DIM)))
    w["attn_w_o"] = p["attn_w_o"].astype(BF16)
    wr = jnp.concatenate([_pad_lanes(p["moe_w_group"], LANES), _pad_lanes(p["moe_w_expert"], LANES)], axis=-1)
    br = jnp.concatenate([_pad_lanes(p["moe_b_group"], LANES), _pad_lanes(p["moe_b_expert"], LANES)], axis=-1)
    w["moe_wr"] = wr.astype(BF16)
    w["moe_br"] = br[:, None, :]
    w["moe_w1"] = p["moe_w1"].astype(BF16)
    w["moe_w3"] = p["moe_w3"].astype(BF16)
    w["moe_w2"] = p["moe_w2"].astype(BF16)
    return w


def _trunk(x, mods, kv_mod, conv_prev, ssm_prev, k_past, v_past, p, w, cfg):
    bsz, seq, _ = x.shape
    tl, lt, lc, tq, tk = cfg["tl"], cfg["lt"], cfg["lc"], cfg["tq"], cfg["tk"]
    pos0 = 0 if k_past is None else k_past.shape[1]
    h = x
    conv_new, ssm_new = [], []
    k_new = v_new = k_all = v_all = None
    for layer in range(DEPTH):
        mod = mods[layer]
        if layer < N_A_LAYERS:
            i = layer
            cprev = jnp.pad(conv_prev[i], ((0, 0), (8 - (CONV_WIDTH - 1), 0), (0, 0)))
            sprev = ssm_prev[i].reshape(bsz, D_INNER, SSM_STATE)
            h, cn, sn = _mamba_layer(
                h, mod, p["norm_g"][layer, 0][None, :], w["ssm_w_in"][i], p["ssm_conv_w"][i],
                p["ssm_conv_b"][i][None, :], w["ssm_dt_bias"][i], w["ssm_a_log"][i], w["ssm_d"][i],
                p["ssm_norm_g"][i][None, :], w["ssm_w_out"][i], cprev, sprev, w["e2"], lt, lc)
            conv_new.append(cn[:, 8 - (CONV_WIDTH - 1):, :])
            ssm_new.append(sn.reshape(bsz, SSM_HEADS, SSM_HEAD_DIM, SSM_STATE))
        else:
            j = layer - N_A_LAYERS
            lam_init = 0.8 - 0.6 * math.exp(-0.3 * layer)
            q = _normproj(h, mod[:, 0:2], p["norm_g"][layer, 0][None, :], w["attn_w_q"][j],
                          w["attn_q_norm_g"][j], tl, False, "attn_q_proj")[0]
            o = _diff_attention(q, k_all, v_all, _alibi_slopes(), w["lam_rows"][j],
                                p["attn_sub_g"][j][None, :], tq, tk, pos0, lam_init)
            h = _proj_res(o, w["attn_w_o"][j], h, mod[:, 2:3], tl)
        u, comb = _moe_route(h, mod[:, 3:5], p["norm_g"][layer, 1][None, :], w["moe_wr"][layer],
                             w["moe_br"][layer], tl)
        h = _moe_dense(u, comb, w["moe_w1"][layer], w["moe_w3"][layer], w["moe_w2"][layer], h, mod[:, 5:6], tl)
        if layer == N_A_LAYERS - 1:
            kv_f32, kv_bf = _normproj(h, kv_mod, p["kv_norm_g"][None, :], w["w_kv"], w["k_norm_g"], tl, True,
                                      "shared_kv")
            nk_cols = DIFF_HEADS * 2 * DIFF_HEAD_DIM
            k_new = kv_f32[..., :nk_cols].reshape(bsz, seq, DIFF_HEADS, 2, DIFF_HEAD_DIM)
            v_new = kv_f32[..., nk_cols:].reshape(bsz, seq, DIFF_HEADS, DIFF_V_DIM)
            k_all, v_all = kv_bf[..., :nk_cols], kv_bf[..., nk_cols:]
            if k_past is not None:
                k_all = jnp.concatenate([k_past.reshape(bsz, pos0, nk_cols).astype(BF16), k_all], axis=1)
                v_all = jnp.concatenate([v_past.reshape(bsz, pos0, nk_cols).astype(BF16), v_all], axis=1)
                pad = (-k_all.shape[1]) % tk
                k_all = jnp.pad(k_all, ((0, 0), (0, pad), (0, 0)))
                v_all = jnp.pad(v_all, ((0, 0), (0, pad), (0, 0)))
    return h, jnp.stack(conv_new), jnp.stack(ssm_new), k_new, v_new


PROMPT_CFG = dict(tl=512, lt=256, lc=64, tq=512, tk=512)
SAMPLE_CFG = dict(tl=64, lt=64, lc=64, tq=64, tk=1152)


def kernel(x_prompt, x_sample, c_prompt, c_sample, state_conv, state_ssm, cache_k, cache_v, ada_w, ada_b, norm_g, ssm_w_in, ssm_conv_w, ssm_conv_b, ssm_dt_bias, ssm_a_log, ssm_d, ssm_norm_g, ssm_w_out, kv_norm_g, kv_ada_w, kv_ada_b, w_kv, k_norm_g, attn_w_q, attn_q_norm_g, attn_lambda_q1, attn_lambda_k1, attn_lambda_q2, attn_lambda_k2, attn_sub_g, attn_w_o, moe_w_group, moe_b_group, moe_w_expert, moe_b_expert, moe_w1, moe_w3, moe_w2):
    p = dict(norm_g=norm_g, ssm_w_in=ssm_w_in, ssm_conv_w=ssm_conv_w, ssm_conv_b=ssm_conv_b,
             ssm_dt_bias=ssm_dt_bias, ssm_a_log=ssm_a_log, ssm_d=ssm_d, ssm_norm_g=ssm_norm_g,
             ssm_w_out=ssm_w_out, kv_norm_g=kv_norm_g, w_kv=w_kv, k_norm_g=k_norm_g, attn_w_q=attn_w_q,
             attn_q_norm_g=attn_q_norm_g, attn_lambda_q1=attn_lambda_q1, attn_lambda_k1=attn_lambda_k1,
             attn_lambda_q2=attn_lambda_q2, attn_lambda_k2=attn_lambda_k2, attn_sub_g=attn_sub_g,
             attn_w_o=attn_w_o, moe_w_group=moe_w_group, moe_b_group=moe_b_group, moe_w_expert=moe_w_expert,
             moe_b_expert=moe_b_expert, moe_w1=moe_w1, moe_w3=moe_w3, moe_w2=moe_w2)
    w = _prep_weights(p)

    bp, bs = x_prompt.shape[0], x_sample.shape[0]
    c_all = jnp.concatenate([c_prompt, c_sample], axis=0)
    c_all = jnp.pad(c_all, ((0, (-c_all.shape[0]) % 8), (0, 0)))
    mods = _mods(c_all, ada_w, ada_b[:, None, :], 1536).reshape(DEPTH, -1, 6, D_MODEL)
    kv_mods = _mods(c_all, kv_ada_w[None], kv_ada_b[None, None, :], 1024).reshape(-1, 2, D_MODEL)

    conv0 = jnp.zeros((N_A_LAYERS, bp, CONV_WIDTH - 1, CONV_DIM), F32)
    ssm0 = jnp.zeros((N_A_LAYERS, bp, SSM_HEADS, SSM_HEAD_DIM, SSM_STATE), F32)
    y_p, conv_p, ssm_p, k_p, v_p = _trunk(x_prompt, mods[:, :bp], kv_mods[:bp], conv0, ssm0, None, None,
                                          p, w, PROMPT_CFG)
    y_s, conv_s, ssm_s, k_s, v_s = _trunk(x_sample, mods[:, bp:bp + bs], kv_mods[bp:bp + bs], state_conv,
                                          state_ssm, cache_k, cache_v, p, w, SAMPLE_CFG)
    return (y_p, y_s, conv_p, ssm_p, k_p, v_p, conv_s, ssm_s, k_s, v_s)
```

```python
import functools
import math

import jax
import jax.numpy as jnp
from jax import lax
from jax.experimental import pallas as pl
from jax.experimental.pallas import tpu as pltpu

F32 = jnp.float32
BF16 = jnp.bfloat16
HIGHEST = lax.Precision.HIGHEST

D_MODEL = 1024
DEPTH = 4
N_A_LAYERS = 2
EPS = 1e-6
CHUNK = 64
D_INNER = 2048
SSM_HEADS = 32
SSM_HEAD_DIM = 64
SSM_GROUPS = 4
SSM_STATE = 128
GROUP_WIDTH = D_INNER // SSM_GROUPS
CONV_WIDTH = 4
CONV_DIM = D_INNER + 2 * SSM_GROUPS * SSM_STATE
DT_PAD = 128
IN_PROJ_PAD = D_INNER + CONV_DIM + DT_PAD
DIFF_HEADS = 8
DIFF_HEAD_DIM = 64
DIFF_V_DIM = 128
N_GROUPS = 4
EXPERTS_PER_GROUP = 8
N_EXPERTS = 32
D_EXPERT = 512

LANES = 128
NEG_BIG = -1e30
FIXED_SHIFT_LIMIT = 40.0
VMEM_LIMIT = 56 * 1024 * 1024


def _params(semantics, vmem=VMEM_LIMIT):
    return pltpu.CompilerParams(dimension_semantics=semantics, vmem_limit_bytes=vmem)


def _silu(x):
    return x * (1.0 / (1.0 + jnp.exp(-x)))


def _softplus(x):
    return jnp.maximum(x, 0.0) + jnp.log1p(jnp.exp(-jnp.abs(x)))


def _mod_norm(h, gain, shift, scale):
    ms = jnp.mean(h * h, axis=-1, keepdims=True)
    return (h * lax.rsqrt(ms + EPS) * gain) * (1.0 + scale) + shift


def _seg_rms64(x, g, n_cols):
    lane = lax.broadcasted_iota(jnp.int32, (x.shape[0], LANES), 1)
    low = lane < 64
    outs = []
    for cb in range(n_cols // LANES):
        xc = x[:, cb * LANES:(cb + 1) * LANES]
        sq = xc * xc
        s_lo = jnp.sum(jnp.where(low, sq, 0.0), axis=-1, keepdims=True)
        s_hi = jnp.sum(jnp.where(low, 0.0, sq), axis=-1, keepdims=True)
        r = jnp.where(low, lax.rsqrt(s_lo * (1.0 / 64) + EPS), lax.rsqrt(s_hi * (1.0 / 64) + EPS))
        outs.append(xc * r * g[:, cb * LANES:(cb + 1) * LANES])
    if n_cols < x.shape[1]:
        outs.append(x[:, n_cols:])
    return jnp.concatenate(outs, axis=1)


def _mods_kernel(c_ref, w_ref, b_ref, o_ref):
    s = _silu(c_ref[...])
    o_ref[0] = jnp.dot(s.astype(BF16), w_ref[0].astype(BF16), preferred_element_type=F32) + b_ref[0]


def _mods(c_all, w, b, tn):
    nl, _, n = w.shape
    m = c_all.shape[0]
    return pl.pallas_call(
        _mods_kernel,
        grid=(nl, n // tn),
        in_specs=[
            pl.BlockSpec((m, D_MODEL), lambda l, j: (0, 0)),
            pl.BlockSpec((1, D_MODEL, tn), lambda l, j: (l, 0, j)),
            pl.BlockSpec((1, 1, tn), lambda l, j: (l, 0, j)),
        ],
        out_specs=pl.BlockSpec((1, m, tn), lambda l, j: (l, 0, j)),
        out_shape=jax.ShapeDtypeStruct((nl, m, n), F32),
        compiler_params=_params(("parallel", "parallel")),
        name="adaln_mods",
    )(c_all, w, b)


def _expand_heads(v, e2_ref):
    hi = v.astype(BF16)
    lo = (v - hi.astype(F32)).astype(BF16)
    return jnp.dot(jnp.concatenate([hi, lo], axis=1), e2_ref[...], preferred_element_type=F32)


def _mamba_kernel(h_ref, mod_ref, ng_ref, w_ref, cw_ref, cb_ref, dtb_ref, alog_ref, dsk_ref, sng_ref,
                  wo_ref, cprev_ref, sprev_ref, e2_ref,
                  hout_ref, cnew_ref, snew_ref,
                  xbc_scr, st_scr, xd_scr, y_scr, *, lt, lc, nc):
    c = pl.program_id(1)

    @pl.when(c == 0)
    def _():
        xbc_scr[0:8, :] = cprev_ref[0]
        st_scr[...] = sprev_ref[0].T

    h = h_ref[0]
    mod = mod_ref[0]
    u = _mod_norm(h, ng_ref[...], mod[0:1], mod[1:2])
    proj = jnp.dot(u.astype(BF16), w_ref[...], preferred_element_type=F32)
    z = proj[:, :D_INNER]
    xbc_scr[8:8 + lt, :] = proj[:, D_INNER:D_INNER + CONV_DIM]
    dt_raw = proj[:, D_INNER + CONV_DIM:]

    cw = cw_ref[...]
    conv = cb_ref[...]
    for k in range(CONV_WIDTH):
        conv = conv + cw[k:k + 1] * xbc_scr[5 + k:5 + k + lt, :]
    tail = xbc_scr[lt:lt + 8, :]
    cnew_ref[0] = tail
    xbc_scr[0:8, :] = tail
    xbc = _silu(conv)
    xs = xbc[:, :D_INNER]
    bm = xbc[:, D_INNER:D_INNER + GROUP_WIDTH]
    cm = xbc[:, D_INNER + GROUP_WIDTH:].astype(BF16)

    dt = _softplus(dt_raw + dtb_ref[...])
    da = dt * (-jnp.exp(alog_ref[...]))
    xd_scr[...] = xs * _expand_heads(dt, e2_ref)

    row_i = lax.broadcasted_iota(jnp.int32, (lc, lc), 0)
    col_j = lax.broadcasted_iota(jnp.int32, (lc, lc), 1)
    tril = col_j <= row_i
    trilf = tril.astype(F32)
    low = lax.broadcasted_iota(jnp.int32, (lc, LANES), 1) < SSM_HEAD_DIM

    for s in range(lt // lc):
        r0 = s * lc
        acum = jnp.dot(trilf, da[r0:r0 + lc], precision=HIGHEST, preferred_element_type=F32)
        acum_t = acum.T
        acum_e = _expand_heads(acum, e2_ref)
        ea = jnp.exp(acum_e)
        dend = jnp.exp(acum_e[lc - 1:lc] - acum_e)
        xd_s = xd_scr[r0:r0 + lc, :]
        xdb = xd_s.astype(BF16)
        xdw = (xd_s * dend).astype(BF16)
        for g in range(SSM_GROUPS):
            gl = slice(g * GROUP_WIDTH, (g + 1) * GROUP_WIDTH)
            b_g = bm[r0:r0 + lc, g * SSM_STATE:(g + 1) * SSM_STATE]
            c_g = cm[r0:r0 + lc, g * SSM_STATE:(g + 1) * SSM_STATE]
            cbm = lax.dot_general(c_g, b_g.astype(BF16), (((1,), (1,)), ((), ())),
                                  preferred_element_type=F32)
            s_g = st_scr[:, gl]
            yoff = jnp.dot(c_g, s_g.astype(BF16), preferred_element_type=F32) * ea[:, gl]
            for j in range(GROUP_WIDTH // LANES):
                pair = g * (GROUP_WIDTH // LANES) + j
                pl_ = slice(pair * LANES, (pair + 1) * LANES)
                ys = []
                for hh in (2 * pair, 2 * pair + 1):
                    seg = acum[:, hh:hh + 1] - acum_t[hh:hh + 1, :]
                    dec = jnp.exp(jnp.where(tril, seg, NEG_BIG))
                    ys.append(jnp.dot((cbm * dec).astype(BF16), xdb[:, pl_], preferred_element_type=F32))
                y_scr[r0:r0 + lc, pl_] = jnp.where(low, ys[0], ys[1]) + yoff[:, j * LANES:(j + 1) * LANES]
            upd = jnp.dot(b_g.T.astype(BF16), xdw[:, gl], preferred_element_type=F32)
            st_scr[:, gl] = s_g * ea[lc - 1:lc, gl] + upd

    y = (y_scr[...] + dsk_ref[...] * xs) * _silu(z)
    parts = []
    for g in range(SSM_GROUPS):
        gl = slice(g * GROUP_WIDTH, (g + 1) * GROUP_WIDTH)
        yg = y[:, gl]
        ms = jnp.mean(yg * yg, axis=-1, keepdims=True)
        parts.append(yg * lax.rsqrt(ms + EPS) * sng_ref[:, gl])
    yn = jnp.concatenate(parts, axis=1).astype(BF16)
    out = jnp.dot(yn, wo_ref[...], preferred_element_type=F32)
    hout_ref[0] = h + mod[2:3] * out

    @pl.when(c == nc - 1)
    def _():
        snew_ref[0] = st_scr[...].T


def _mamba_layer(h, mod, ng, w_in, cw, cb, dtb, alog, dsk, sng, wo, cprev, sprev, e2, lt, lc):
    bsz, seq, _ = h.shape
    nc = seq // lt
    const2 = lambda b, c: (0, 0)
    perb = lambda b, c: (b, 0, 0)
    kern = functools.partial(_mamba_kernel, lt=lt, lc=lc, nc=nc)
    return pl.pallas_call(
        kern,
        grid=(bsz, nc),
        in_specs=[
            pl.BlockSpec((1, lt, D_MODEL), lambda b, c: (b, c, 0)),
            pl.BlockSpec((1, 6, D_MODEL), perb),
            pl.BlockSpec((1, D_MODEL), const2),
            pl.BlockSpec((D_MODEL, IN_PROJ_PAD), const2),
            pl.BlockSpec((CONV_WIDTH, CONV_DIM), const2),
            pl.BlockSpec((1, CONV_DIM), const2),
            pl.BlockSpec((1, DT_PAD), const2),
            pl.BlockSpec((1, DT_PAD), const2),
            pl.BlockSpec((1, D_INNER), const2),
            pl.BlockSpec((1, D_INNER), const2),
            pl.BlockSpec((D_INNER, D_MODEL), const2),
            pl.BlockSpec((1, 8, CONV_DIM), perb),
            pl.BlockSpec((1, D_INNER, SSM_STATE), perb),
            pl.BlockSpec((2 * DT_PAD, D_INNER), const2),
        ],
        out_specs=[
            pl.BlockSpec((1, lt, D_MODEL), lambda b, c: (b, c, 0)),
            pl.BlockSpec((1, 8, CONV_DIM), perb),
            pl.BlockSpec((1, D_INNER, SSM_STATE), perb),
        ],
        out_shape=[
            jax.ShapeDtypeStruct((bsz, seq, D_MODEL), F32),
            jax.ShapeDtypeStruct((bsz, 8, CONV_DIM), F32),
            jax.ShapeDtypeStruct((bsz, D_INNER, SSM_STATE), F32),
        ],
        scratch_shapes=[
            pltpu.VMEM((lt + 8, CONV_DIM), F32),
            pltpu.VMEM((SSM_STATE, D_INNER), F32),
            pltpu.VMEM((lt, D_INNER), F32),
            pltpu.VMEM((lt, D_INNER), F32),
        ],
        compiler_params=_params(("parallel", "arbitrary")),
        name="mamba_layer",
    )(h, mod, ng, w_in, cw, cb, dtb, alog, dsk, sng, wo, cprev, sprev, e2)


def _normproj_kernel(h_ref, mod_ref, ng_ref, w_ref, sg_ref, *out_refs, n_norm, want_f32):
    mod = mod_ref[0]
    u = _mod_norm(h_ref[0], ng_ref[...], mod[0:1], mod[1:2])
    y = jnp.dot(u.astype(BF16), w_ref[...], preferred_element_type=F32)
    y = _seg_rms64(y, sg_ref[...], n_norm)
    if want_f32:
        out_refs[0][0] = y
    out_refs[-1][0] = y.astype(BF16)


def _normproj(h, mod2, ng, w, seg_g, tl, want_f32, name):
    bsz, seq, _ = h.shape
    nout = w.shape[1]
    n_norm = seg_g.shape[1]
    tok = lambda b, t: (b, t, 0)
    const2 = lambda b, t: (0, 0)
    out_specs = [pl.BlockSpec((1, tl, nout), tok)]
    out_shape = [jax.ShapeDtypeStruct((bsz, seq, nout), BF16)]
    if want_f32:
        out_specs = [pl.BlockSpec((1, tl, nout), tok)] + out_specs
        out_shape = [jax.ShapeDtypeStruct((bsz, seq, nout), F32)] + out_shape
    return pl.pallas_call(
        functools.partial(_normproj_kernel, n_norm=n_norm, want_f32=want_f32),
        grid=(bsz, seq // tl),
        in_specs=[
            pl.BlockSpec((1, tl, D_MODEL), tok),
            pl.BlockSpec((1, 2, D_MODEL), lambda b, t: (b, 0, 0)),
            pl.BlockSpec((1, D_MODEL), const2),
            pl.BlockSpec((D_MODEL, nout), const2),
            pl.BlockSpec((1, n_norm), const2),
        ],
        out_specs=out_specs,
        out_shape=out_shape,
        compiler_params=_params(("parallel", "parallel")),
        name=name,
    )(h, mod2, ng, w, seg_g)


def _attn_kernel(slope_ref, q_ref, k_ref, v_ref, lam_ref, sg_ref, o_ref,
                 m1, l1, a1, m2, l2, a2, *, tq, tk, nk, pos0, lam_init):
    hd = pl.program_id(1)
    qi = pl.program_id(2)
    ki = pl.program_id(3)

    @pl.when(ki == 0)
    def _():
        for m, l, a in ((m1, l1, a1), (m2, l2, a2)):
            m[...] = jnp.full(m.shape, NEG_BIG, F32)
            l[...] = jnp.zeros(l.shape, F32)
            a[...] = jnp.zeros(a.shape, F32)

    q_first = pos0 + qi * tq
    last_q_chunk = (q_first + tq - 1) // CHUNK
    first_k_chunk = (ki * tk) // CHUNK

    @pl.when(first_k_chunk <= last_q_chunk)
    def _():
        q = q_ref[0]
        k = k_ref[0]
        v = v_ref[0]
        lane = lax.broadcasted_iota(jnp.int32, q.shape, 1)
        zero = jnp.zeros_like(q)
        qa = jnp.where(lane < DIFF_HEAD_DIM, q, zero)
        qb = jnp.where(lane < DIFF_HEAD_DIM, zero, q)
        qpos = q_first + lax.broadcasted_iota(jnp.int32, (tq, tk), 0)
        kpos = ki * tk + lax.broadcasted_iota(jnp.int32, (tq, tk), 1)
        visible = (kpos // CHUNK) <= (qpos // CHUNK)
        dist = jnp.abs(qpos - kpos).astype(F32)
        bias = jnp.where(visible, -slope_ref[hd] * dist, NEG_BIG)
        scale = DIFF_HEAD_DIM ** -0.5
        nt = (((1,), (1,)), ((), ()))
        for qq, m, l, a in ((qa, m1, l1, a1), (qb, m2, l2, a2)):
            s = lax.dot_general(qq, k, nt, preferred_element_type=F32) * scale + bias
            m_old = m[...]
            m_new = jnp.maximum(m_old, jnp.max(s, axis=-1, keepdims=True))
            p = jnp.exp(s - m_new)
            alpha = jnp.exp(m_old - m_new)
            l[...] = alpha * l[...] + jnp.sum(p, axis=-1, keepdims=True)
            a[...] = alpha * a[...] + jnp.dot(p.astype(BF16), v, preferred_element_type=F32)
            m[...] = m_new

    @pl.when(ki == nk - 1)
    def _():
        lp = lam_ref[...]
        lam = (jnp.exp(jnp.sum(lp[0:1] * lp[1:2], axis=-1, keepdims=True))
               - jnp.exp(jnp.sum(lp[2:3] * lp[3:4], axis=-1, keepdims=True)) + lam_init)
        o = a1[...] / l1[...] - lam * (a2[...] / l2[...])
        ms = jnp.mean(o * o, axis=-1, keepdims=True)
        o_ref[0] = (o * lax.rsqrt(ms + EPS) * sg_ref[...] * (1.0 - lam_init)).astype(BF16)


def _diff_attention(q, k, v, slopes, lam_rows, sub_g, tq, tk, pos0, lam_init):
    bsz, lq, _ = q.shape
    lk = k.shape[1]
    nq, nk = lq // tq, lk // tk

    def kv_map(b, h, i, j, slope_ref):
        last_visible = ((pos0 + i * tq + tq - 1) // CHUNK * CHUNK + CHUNK - 1) // tk
        return (b, jnp.minimum(j, last_visible), h)

    grid_spec = pltpu.PrefetchScalarGridSpec(
        num_scalar_prefetch=1,
        grid=(bsz, DIFF_HEADS, nq, nk),
        in_specs=[
            pl.BlockSpec((1, tq, LANES), lambda b, h, i, j, s: (b, i, h)),
            pl.BlockSpec((1, tk, LANES), kv_map),
            pl.BlockSpec((1, tk, LANES), kv_map),
            pl.BlockSpec((8, LANES), lambda b, h, i, j, s: (0, 0)),
            pl.BlockSpec((1, LANES), lambda b, h, i, j, s: (0, 0)),
        ],
        out_specs=pl.BlockSpec((1, tq, LANES), lambda b, h, i, j, s: (b, i, h)),
        scratch_shapes=[
            pltpu.VMEM((tq, 1), F32), pltpu.VMEM((tq, 1), F32), pltpu.VMEM((tq, LANES), F32),
            pltpu.VMEM((tq, 1), F32), pltpu.VMEM((tq, 1), F32), pltpu.VMEM((tq, LANES), F32),
        ],
    )
    return pl.pallas_call(
        functools.partial(_attn_kernel, tq=tq, tk=tk, nk=nk, pos0=pos0, lam_init=lam_init),
        grid_spec=grid_spec,
        out_shape=jax.ShapeDtypeStruct((bsz, lq, DIFF_HEADS * DIFF_V_DIM), BF16),
        compiler_params=_params(("parallel", "parallel", "parallel", "arbitrary")),
        name="diff_attention",
    )(slopes, q, k, v, lam_rows, sub_g)


def _attn_fixed_kernel(sc_ref, q_ref, kt_ref, v_ref, lam_ref, sg_ref, o_ref, acc1, acc2, *, t, lam_init):
    hd = pl.program_id(1)
    qi = pl.program_id(2)
    slope = sc_ref[hd]
    shift = sc_ref[DIFF_HEADS]

    lane = lax.broadcasted_iota(jnp.int32, (t, LANES), 1)
    qoff = lax.broadcasted_iota(jnp.int32, (t, LANES), 0)
    q_hi = (qoff >> 8).astype(F32) * 256.0
    q_lo = (qoff & 255).astype(F32)
    aug_q = jnp.where(lane == 0, -slope * q_hi, jnp.where(lane == 1, -slope * q_lo, jnp.where(
        (lane == 2) | (lane == 3), 1.0, jnp.where(lane == 4, -shift, 0.0)))).astype(BF16)
    qs = q_ref[0] * jnp.asarray(DIFF_HEAD_DIM ** -0.5, BF16)
    zero = jnp.zeros_like(qs)
    qa = jnp.concatenate([jnp.where(lane < DIFF_HEAD_DIM, qs, zero), aug_q], axis=1)
    qb = jnp.concatenate([jnp.where(lane < DIFF_HEAD_DIM, zero, qs), aug_q], axis=1)
    acc1[...] = jnp.zeros(acc1.shape, F32)
    acc2[...] = jnp.zeros(acc2.shape, F32)
    ones = jnp.ones((t, LANES), BF16)
    aug_row = lax.broadcasted_iota(jnp.int32, (16, t), 0)
    koff = lax.broadcasted_iota(jnp.int32, (16, t), 1)
    pad_rows = jnp.zeros((LANES - 16, t), BF16)

    def accumulate(kb, aug_k, bias):
        k0 = pl.multiple_of(kb * t, t)
        kfull = jnp.concatenate([kt_ref[0, 0, kb], aug_k.astype(BF16), pad_rows], axis=0)
        v2 = jnp.concatenate([v_ref[0, pl.ds(k0, t), :], ones], axis=1)
        for qq, acc in ((qa, acc1), (qb, acc2)):
            s = jnp.dot(qq, kfull, preferred_element_type=F32)
            if bias is not None:
                s = s + bias
            acc[...] += jnp.dot(jnp.exp(s).astype(BF16), v2, preferred_element_type=F32)

    def below_diagonal(kb, carry):
        rel = (kb - qi) * t + koff
        k_hi = (rel >> 8).astype(F32) * 256.0
        k_lo = (rel & 255).astype(F32)
        aug_k = jnp.where(aug_row <= 1, 1.0, jnp.where(aug_row == 2, slope * k_hi, jnp.where(
            aug_row == 3, slope * k_lo, jnp.where(aug_row == 4, 1.0, 0.0))))
        accumulate(kb, aug_k, None)
        return carry

    lax.fori_loop(0, qi, below_diagonal, 0)

    qo = lax.broadcasted_iota(jnp.int32, (t, t), 0)
    ko = lax.broadcasted_iota(jnp.int32, (t, t), 1)
    visible = (ko // CHUNK) <= (qo // CHUNK)
    bias = jnp.where(visible, -slope * jnp.abs(qo - ko).astype(F32), NEG_BIG)
    accumulate(qi, jnp.where(aug_row == 4, 1.0, 0.0), bias)

    lp = lam_ref[...]
    lam = (jnp.exp(jnp.sum(lp[0:1] * lp[1:2], axis=-1, keepdims=True))
           - jnp.exp(jnp.sum(lp[2:3] * lp[3:4], axis=-1, keepdims=True)) + lam_init)
    a1 = acc1[...]
    a2 = acc2[...]
    o = a1[:, :LANES] / a1[:, LANES:] - lam * (a2[:, :LANES] / a2[:, LANES:])
    ms = jnp.mean(o * o, axis=-1, keepdims=True)
    o_ref[0] = (o * lax.rsqrt(ms + EPS) * sg_ref[...] * (1.0 - lam_init)).astype(BF16)


def _diff_attention_fixed(q, k, v, slopes_shift, lam_rows, sub_g, t, lam_init):
    bsz, seq, _ = q.shape
    nb = seq // t
    kt = k.reshape(bsz, nb, t, DIFF_HEADS, LANES).transpose(0, 3, 1, 4, 2)
    grid_spec = pltpu.PrefetchScalarGridSpec(
        num_scalar_prefetch=1,
        grid=(bsz, DIFF_HEADS, nb),
        in_specs=[
            pl.BlockSpec((1, t, LANES), lambda b, h, i, s: (b, i, h)),
            pl.BlockSpec((1, 1, nb, LANES, t), lambda b, h, i, s: (b, h, 0, 0, 0)),
            pl.BlockSpec((1, seq, LANES), lambda b, h, i, s: (b, 0, h)),
            pl.BlockSpec((8, LANES), lambda b, h, i, s: (0, 0)),
            pl.BlockSpec((1, LANES), lambda b, h, i, s: (0, 0)),
        ],
        out_specs=pl.BlockSpec((1, t, LANES), lambda b, h, i, s: (b, i, h)),
        scratch_shapes=[pltpu.VMEM((t, 2 * LANES), F32), pltpu.VMEM((t, 2 * LANES), F32)],
    )
    return pl.pallas_call(
        functools.partial(_attn_fixed_kernel, t=t, lam_init=lam_init),
        grid_spec=grid_spec,
        out_shape=jax.ShapeDtypeStruct((bsz, seq, DIFF_HEADS * DIFF_V_DIM), BF16),
        compiler_params=_params(("parallel", "parallel", "arbitrary")),
        name="diff_attention_fixed",
    )(slopes_shift, q, kt, v, lam_rows, sub_g)


def _proj_res_kernel(x_ref, w_ref, h_ref, gate_ref, o_ref):
    y = jnp.dot(x_ref[0], w_ref[...], preferred_element_type=F32)
    o_ref[0] = h_ref[0] + gate_ref[0] * y


def _proj_res(x, w, h, gate, tl):
    bsz, seq, kdim = x.shape
    tok = lambda b, t: (b, t, 0)
    return pl.pallas_call(
        _proj_res_kernel,
        grid=(bsz, seq // tl),
        in_specs=[
            pl.BlockSpec((1, tl, kdim), tok),
            pl.BlockSpec((kdim, D_MODEL), lambda b, t: (0, 0)),
            pl.BlockSpec((1, tl, D_MODEL), tok),
            pl.BlockSpec((1, 1, D_MODEL), lambda b, t: (b, 0, 0)),
        ],
        out_specs=pl.BlockSpec((1, tl, D_MODEL), tok),
        out_shape=jax.ShapeDtypeStruct((bsz, seq, D_MODEL), F32),
        compiler_params=_params(("parallel", "parallel")),
        name="attn_out_proj",
    )(x, w, h, gate)


def _moe_route_kernel(h_ref, mod_ref, ng_ref, wr_ref, br_ref, u_ref, route_ref):
    mod = mod_ref[0]
    u = _mod_norm(h_ref[0], ng_ref[...], mod[0:1], mod[1:2])
    u_ref[0] = u
    logits = jnp.dot(u.astype(BF16), wr_ref[...], preferred_element_type=F32) + br_ref[...]
    gl = logits[:, :LANES]
    el = logits[:, LANES:]
    lane = lax.broadcasted_iota(jnp.int32, gl.shape, 1)
    lane_f = lane.astype(F32)
    none = float(LANES)

    gmask = lane < N_GROUPS
    glm = jnp.where(gmask, gl, NEG_BIG)
    gmax = jnp.max(glm, axis=-1, keepdims=True)
    g_idx = jnp.min(jnp.where(glm == gmax, lane_f, none), axis=-1, keepdims=True)
    g_w = 1.0 / jnp.sum(jnp.where(gmask, jnp.exp(glm - gmax), 0.0), axis=-1, keepdims=True)

    lo = g_idx * EXPERTS_PER_GROUP
    emask = (lane_f >= lo) & (lane_f < lo + EXPERTS_PER_GROUP)
    elm = jnp.where(emask, el, NEG_BIG)
    emax = jnp.max(elm, axis=-1, keepdims=True)
    ee = jnp.where(emask, jnp.exp(elm - emax), 0.0)
    ep = ee / jnp.sum(ee, axis=-1, keepdims=True)
    epm = jnp.where(emask, ep, -1.0)
    p1 = jnp.max(epm, axis=-1, keepdims=True)
    i1 = jnp.min(jnp.where(epm == p1, lane_f, none), axis=-1, keepdims=True)
    epm2 = jnp.where(lane_f == i1, -1.0, epm)
    p2 = jnp.max(epm2, axis=-1, keepdims=True)
    i2 = jnp.min(jnp.where(epm2 == p2, lane_f, none), axis=-1, keepdims=True)
    denom = p1 + p2
    route_ref[0] = jnp.where(lane == 0, i1, jnp.where(lane == 1, i2, jnp.where(
        lane == 2, p1 / denom * g_w, jnp.where(lane == 3, p2 / denom * g_w, 0.0))))


def _moe_route(h, mod2, ng, wr, br, tl):
    bsz, seq, _ = h.shape
    tok = lambda b, t: (b, t, 0)
    const2 = lambda b, t: (0, 0)
    return pl.pallas_call(
        _moe_route_kernel,
        grid=(bsz, seq // tl),
        in_specs=[
            pl.BlockSpec((1, tl, D_MODEL), tok),
            pl.BlockSpec((1, 2, D_MODEL), lambda b, t: (b, 0, 0)),
            pl.BlockSpec((1, D_MODEL), const2),
            pl.BlockSpec((D_MODEL, 2 * LANES), const2),
            pl.BlockSpec((1, 2 * LANES), const2),
        ],
        out_specs=[pl.BlockSpec((1, tl, D_MODEL), tok), pl.BlockSpec((1, tl, LANES), tok)],
        out_shape=[jax.ShapeDtypeStruct((bsz, seq, D_MODEL), F32),
                   jax.ShapeDtypeStruct((bsz, seq, LANES), F32)],
        compiler_params=_params(("parallel", "parallel")),
        name="moe_route",
    )(h, mod2, ng, wr, br)


def _moe_schedule(route, tm):
    t = route.shape[0]
    a = 2 * t
    n_tiles = a // tm + N_EXPERTS
    ea = route[:, :2].astype(jnp.int32).reshape(a)
    _, order = lax.sort((ea, jnp.arange(a, dtype=jnp.int32)), num_keys=1, is_stable=True)
    counts = jnp.sum((ea[:, None] == jnp.arange(N_EXPERTS, dtype=jnp.int32)[None, :]).astype(jnp.int32), axis=0)
    padded = (counts + tm - 1) // tm * tm
    pad_end = jnp.cumsum(padded)
    pad_start = pad_end - padded
    start = jnp.cumsum(counts) - counts
    tile_first = jnp.arange(n_tiles, dtype=jnp.int32) * tm
    tile_e = jnp.minimum(jnp.searchsorted(pad_end, tile_first, side="right"), N_EXPERTS - 1).astype(jnp.int32)
    tile_valid = (tile_first < pad_end[-1]).astype(jnp.int32)
    slot = jnp.arange(n_tiles * tm, dtype=jnp.int32)
    e_s = jnp.repeat(tile_e, tm)
    local = slot - pad_start[e_s]
    valid = local < counts[e_s]
    asg = order[jnp.clip(start[e_s] + local, 0, a - 1)]
    tok = jnp.where(valid, asg // 2, 0)
    dump = 2 * t + ((slot // tm) % 2) * tm + slot % tm
    dest = jnp.where(valid, (asg % 2) * t + asg // 2, dump)
    return tile_e, tile_valid, tok.reshape(n_tiles, 1, tm), dest.reshape(n_tiles, 1, tm)


def _moe_ffn_kernel(te_ref, tv_ref, tok_ref, tokn_ref, dest_ref, w1_ref, w3_ref, w2_ref, u_hbm, y_hbm,
                    xbuf, ybuf, gsem, ssem, *, tm, n_tiles):
    i = pl.program_id(0)
    slot = lax.rem(i, 2)

    def gather_rows(idx_ref, s):
        def body(r, c):
            pltpu.make_async_copy(u_hbm.at[pl.ds(idx_ref[0, 0, r], 1)], xbuf.at[s, pl.ds(r, 1)],
                                  gsem.at[s]).start()
            return c
        lax.fori_loop(0, tm, body, 0)

    def wait_gather(s):
        pltpu.make_async_copy(u_hbm.at[pl.ds(0, tm)], xbuf.at[s], gsem.at[s]).wait()

    def wait_scatter(s):
        pltpu.make_async_copy(ybuf.at[s], y_hbm.at[pl.ds(0, tm)], ssem.at[s]).wait()

    @pl.when(i == 0)
    def _():
        ybuf[...] = jnp.zeros(ybuf.shape, F32)
        n_real = y_hbm.shape[0] - 2 * tm
        for s in range(2):
            fill = pltpu.make_async_copy(ybuf.at[s], y_hbm.at[pl.ds(n_real + s * tm, tm)], ssem.at[s])
            fill.start()
            fill.wait()

    @pl.when((i == 0) & (tv_ref[0] == 1))
    def _():
        gather_rows(tok_ref, 0)

    @pl.when((i + 1 < n_tiles) & (tv_ref[jnp.minimum(i + 1, n_tiles - 1)] == 1))
    def _():
        gather_rows(tokn_ref, 1 - slot)

    @pl.when((i >= 2) & (tv_ref[jnp.maximum(i - 2, 0)] == 1))
    def _():
        wait_scatter(slot)

    @pl.when(tv_ref[i] == 1)
    def _():
        wait_gather(slot)
        x = xbuf[slot].astype(BF16)
        hid = _silu(jnp.dot(x, w1_ref[0], preferred_element_type=F32)) * jnp.dot(
            x, w3_ref[0], preferred_element_type=F32)
        ybuf[slot] = jnp.dot(hid.astype(BF16), w2_ref[0], preferred_element_type=F32)

        def body(r, c):
            pltpu.make_async_copy(ybuf.at[slot, pl.ds(r, 1)], y_hbm.at[pl.ds(dest_ref[0, 0, r], 1)],
                                  ssem.at[slot]).start()
            return c
        lax.fori_loop(0, tm, body, 0)

    @pl.when(i == n_tiles - 1)
    def _():
        @pl.when(tv_ref[n_tiles - 2] == 1)
        def _():
            wait_scatter(1 - slot)

        @pl.when(tv_ref[n_tiles - 1] == 1)
        def _():
            wait_scatter(slot)


def _moe_ffn(u2d, tile_e, tile_valid, tok, dest, w1, w3, w2, tm):
    t = u2d.shape[0]
    n_tiles = tok.shape[0]
    smem_tile = lambda f: pl.BlockSpec((1, 1, tm), f, memory_space=pltpu.SMEM)
    wspec = lambda shp: pl.BlockSpec(shp, lambda i, te, tv: (te[i], 0, 0))
    grid_spec = pltpu.PrefetchScalarGridSpec(
        num_scalar_prefetch=2,
        grid=(n_tiles,),
        in_specs=[
            smem_tile(lambda i, te, tv: (i, 0, 0)),
            smem_tile(lambda i, te, tv: (jnp.minimum(i + 1, n_tiles - 1), 0, 0)),
            smem_tile(lambda i, te, tv: (i, 0, 0)),
            wspec((1, D_MODEL, D_EXPERT)),
            wspec((1, D_MODEL, D_EXPERT)),
            wspec((1, D_EXPERT, D_MODEL)),
            pl.BlockSpec(memory_space=pl.ANY),
        ],
        out_specs=pl.BlockSpec(memory_space=pl.ANY),
        scratch_shapes=[
            pltpu.VMEM((2, tm, D_MODEL), F32),
            pltpu.VMEM((2, tm, D_MODEL), F32),
            pltpu.SemaphoreType.DMA((2,)),
            pltpu.SemaphoreType.DMA((2,)),
        ],
    )
    return pl.pallas_call(
        functools.partial(_moe_ffn_kernel, tm=tm, n_tiles=n_tiles),
        grid_spec=grid_spec,
        out_shape=jax.ShapeDtypeStruct((2 * t + 2 * tm, D_MODEL), F32),
        compiler_params=_params(("arbitrary",)),
        name="moe_ffn",
    )(tile_e, tile_valid, tok, tok, dest, w1, w3, w2, u2d)


def _moe_combine_kernel(h_ref, gate_ref, route_ref, ya_ref, yb_ref, o_ref):
    route = route_ref[0]
    lane = lax.broadcasted_iota(jnp.int32, route.shape, 1)
    wa = jnp.sum(jnp.where(lane == 2, route, 0.0), axis=-1, keepdims=True)
    wb = jnp.sum(jnp.where(lane == 3, route, 0.0), axis=-1, keepdims=True)
    o_ref[0] = h_ref[0] + gate_ref[0] * (wa * ya_ref[...] + wb * yb_ref[...])


def _moe_combine(h, gate, route, y2, tl):
    bsz, seq, _ = h.shape
    nt = seq // tl
    tok = lambda b, t: (b, t, 0)
    return pl.pallas_call(
        _moe_combine_kernel,
        grid=(bsz, nt),
        in_specs=[
            pl.BlockSpec((1, tl, D_MODEL), tok),
            pl.BlockSpec((1, 1, D_MODEL), lambda b, t: (b, 0, 0)),
            pl.BlockSpec((1, tl, LANES), tok),
            pl.BlockSpec((tl, D_MODEL), lambda b, t: (b * nt + t, 0)),
            pl.BlockSpec((tl, D_MODEL), lambda b, t: (bsz * nt + b * nt + t, 0)),
        ],
        out_specs=pl.BlockSpec((1, tl, D_MODEL), tok),
        out_shape=jax.ShapeDtypeStruct((bsz, seq, D_MODEL), F32),
        compiler_params=_params(("parallel", "parallel")),
        name="moe_combine",
    )(h, gate, route, y2, y2)


def _alibi_slopes():
    start = 2.0 ** (-8.0 / DIFF_HEADS)
    return jnp.asarray([start ** (i + 1) for i in range(DIFF_HEADS)], F32)


def _head_expand_matrix():
    head_of_lane = jnp.arange(D_INNER) // SSM_HEAD_DIM
    e = (jnp.arange(DT_PAD)[:, None] == head_of_lane[None, :]).astype(BF16)
    return jnp.concatenate([e, e], axis=0)


def _pad_lanes(x, width):
    return jnp.pad(x, [(0, 0)] * (x.ndim - 1) + [(0, width - x.shape[-1])])


def _prep_weights(p):
    w = {}
    w_in = p["ssm_w_in"]
    w["ssm_w_in"] = jnp.concatenate(
        [w_in[..., :D_INNER + CONV_DIM], _pad_lanes(w_in[..., D_INNER + CONV_DIM:], DT_PAD)], axis=-1).astype(BF16)
    w["ssm_dt_bias"] = _pad_lanes(p["ssm_dt_bias"], DT_PAD)[:, None, :]
    w["ssm_a_log"] = _pad_lanes(p["ssm_a_log"], DT_PAD)[:, None, :]
    w["ssm_d"] = jnp.repeat(p["ssm_d"], SSM_HEAD_DIM, axis=-1)[:, None, :]
    w["ssm_w_out"] = p["ssm_w_out"].astype(BF16)
    w["e2"] = _head_expand_matrix()
    w["w_kv"] = p["w_kv"].astype(BF16)
    w["k_norm_g"] = jnp.tile(p["k_norm_g"], 2 * DIFF_HEADS)[None, :]
    w["attn_w_q"] = p["attn_w_q"].astype(BF16)
    w["attn_q_norm_g"] = jnp.tile(p["attn_q_norm_g"], (1, 2 * DIFF_HEADS))[:, None, :]
    lam = jnp.stack([p["attn_lambda_q1"], p["attn_lambda_k1"], p["attn_lambda_q2"], p["attn_lambda_k2"]], axis=1)
    w["lam_rows"] = jnp.pad(lam, ((0, 0), (0, 4), (0, LANES - DIFF_HEAD_DIM)))
    w["attn_w_o"] = p["attn_w_o"].astype(BF16)
    wr = jnp.concatenate([_pad_lanes(p["moe_w_group"], LANES), _pad_lanes(p["moe_w_expert"], LANES)], axis=-1)
    br = jnp.concatenate([_pad_lanes(p["moe_b_group"], LANES), _pad_lanes(p["moe_b_expert"], LANES)], axis=-1)
    w["moe_wr"] = wr.astype(BF16)
    w["moe_br"] = br[:, None, :]
    w["moe_w1"] = p["moe_w1"].astype(BF16)
    w["moe_w3"] = p["moe_w3"].astype(BF16)
    w["moe_w2"] = p["moe_w2"].astype(BF16)
    return w


def _trunk(x, mods, kv_mod, conv_prev, ssm_prev, k_past, v_past, p, w, cfg):
    bsz, seq, _ = x.shape
    tl, lt, lc, tq, tk = cfg["tl"], cfg["lt"], cfg["lc"], cfg["tq"], cfg["tk"]
    pos0 = 0 if k_past is None else k_past.shape[1]
    h = x
    conv_new, ssm_new = [], []
    k_new = v_new = k_all = v_all = None
    for layer in range(DEPTH):
        mod = mods[layer]
        if layer < N_A_LAYERS:
            i = layer
            cprev = jnp.pad(conv_prev[i], ((0, 0), (8 - (CONV_WIDTH - 1), 0), (0, 0)))
            sprev = ssm_prev[i].reshape(bsz, D_INNER, SSM_STATE)
            h, cn, sn = _mamba_layer(
                h, mod, p["norm_g"][layer, 0][None, :], w["ssm_w_in"][i], p["ssm_conv_w"][i],
                p["ssm_conv_b"][i][None, :], w["ssm_dt_bias"][i], w["ssm_a_log"][i], w["ssm_d"][i],
                p["ssm_norm_g"][i][None, :], w["ssm_w_out"][i], cprev, sprev, w["e2"], lt, lc)
            conv_new.append(cn[:, 8 - (CONV_WIDTH - 1):, :])
            ssm_new.append(sn.reshape(bsz, SSM_HEADS, SSM_HEAD_DIM, SSM_STATE))
        else:
            j = layer - N_A_LAYERS
            lam_init = 0.8 - 0.6 * math.exp(-0.3 * layer)
            q = _normproj(h, mod[:, 0:2], p["norm_g"][layer, 0][None, :], w["attn_w_q"][j],
                          w["attn_q_norm_g"][j], tl, False, "attn_q_proj")[0]
            slopes = _alibi_slopes()
            sub_g = p["attn_sub_g"][j][None, :]
            online = functools.partial(_diff_attention, slopes=slopes, lam_rows=w["lam_rows"][j], sub_g=sub_g,
                                       tq=tq, tk=tk, pos0=pos0, lam_init=lam_init)
            if k_past is None and tq == tk:
                bound = (DIFF_HEAD_DIM ** 0.5 * (1.0 + 2.0 ** -6) * jnp.max(jnp.abs(p["attn_q_norm_g"][j]))
                         * jnp.max(jnp.abs(p["k_norm_g"])))
                slopes_shift = jnp.concatenate([slopes, bound[None], jnp.zeros((7,), F32)])
                o = lax.cond(
                    bound <= FIXED_SHIFT_LIMIT,
                    lambda q_, k_, v_: _diff_attention_fixed(q_, k_, v_, slopes_shift, w["lam_rows"][j], sub_g,
                                                             tq, lam_init),
                    lambda q_, k_, v_: online(q_, k_, v_),
                    q, k_all, v_all)
            else:
                o = online(q, k_all, v_all)
            h = _proj_res(o, w["attn_w_o"][j], h, mod[:, 2:3], tl)
        u, route = _moe_route(h, mod[:, 3:5], p["norm_g"][layer, 1][None, :], w["moe_wr"][layer],
                              w["moe_br"][layer], tl)
        tile_e, tile_valid, tok, dest = _moe_schedule(route.reshape(bsz * seq, LANES), cfg["tm"])
        y2 = _moe_ffn(u.reshape(bsz * seq, D_MODEL), tile_e, tile_valid, tok, dest, w["moe_w1"][layer],
                      w["moe_w3"][layer], w["moe_w2"][layer], cfg["tm"])
        h = _moe_combine(h, mod[:, 5:6], route, y2, tl)
        if layer == N_A_LAYERS - 1:
            kv_f32, kv_bf = _normproj(h, kv_mod, p["kv_norm_g"][None, :], w["w_kv"], w["k_norm_g"], tl, True,
                                      "shared_kv")
            nk_cols = DIFF_HEADS * 2 * DIFF_HEAD_DIM
            k_new = kv_f32[..., :nk_cols].reshape(bsz, seq, DIFF_HEADS, 2, DIFF_HEAD_DIM)
            v_new = kv_f32[..., nk_cols:].reshape(bsz, seq, DIFF_HEADS, DIFF_V_DIM)
            k_all, v_all = kv_bf[..., :nk_cols], kv_bf[..., nk_cols:]
            if k_past is not None:
                k_all = jnp.concatenate([k_past.reshape(bsz, pos0, nk_cols).astype(BF16), k_all], axis=1)
                v_all = jnp.concatenate([v_past.reshape(bsz, pos0, nk_cols).astype(BF16), v_all], axis=1)
                pad = (-k_all.shape[1]) % tk
                k_all = jnp.pad(k_all, ((0, 0), (0, pad), (0, 0)))
                v_all = jnp.pad(v_all, ((0, 0), (0, pad), (0, 0)))
    return h, jnp.stack(conv_new), jnp.stack(ssm_new), k_new, v_new


PROMPT_CFG = dict(tl=512, lt=256, lc=64, tq=512, tk=512, tm=256)
SAMPLE_CFG = dict(tl=64, lt=64, lc=64, tq=64, tk=1152, tm=64)


def kernel(x_prompt, x_sample, c_prompt, c_sample, state_conv, state_ssm, cache_k, cache_v, ada_w, ada_b, norm_g, ssm_w_in, ssm_conv_w, ssm_conv_b, ssm_dt_bias, ssm_a_log, ssm_d, ssm_norm_g, ssm_w_out, kv_norm_g, kv_ada_w, kv_ada_b, w_kv, k_norm_g, attn_w_q, attn_q_norm_g, attn_lambda_q1, attn_lambda_k1, attn_lambda_q2, attn_lambda_k2, attn_sub_g, attn_w_o, moe_w_group, moe_b_group, moe_w_expert, moe_b_expert, moe_w1, moe_w3, moe_w2):
    p = dict(norm_g=norm_g, ssm_w_in=ssm_w_in, ssm_conv_w=ssm_conv_w, ssm_conv_b=ssm_conv_b,
             ssm_dt_bias=ssm_dt_bias, ssm_a_log=ssm_a_log, ssm_d=ssm_d, ssm_norm_g=ssm_norm_g,
             ssm_w_out=ssm_w_out, kv_norm_g=kv_norm_g, w_kv=w_kv, k_norm_g=k_norm_g, attn_w_q=attn_w_q,
             attn_q_norm_g=attn_q_norm_g, attn_lambda_q1=attn_lambda_q1, attn_lambda_k1=attn_lambda_k1,
             attn_lambda_q2=attn_lambda_q2, attn_lambda_k2=attn_lambda_k2, attn_sub_g=attn_sub_g,
             attn_w_o=attn_w_o, moe_w_group=moe_w_group, moe_b_group=moe_b_group, moe_w_expert=moe_w_expert,
             moe_b_expert=moe_b_expert, moe_w1=moe_w1, moe_w3=moe_w3, moe_w2=moe_w2)
    w = _prep_weights(p)

    bp, bs = x_prompt.shape[0], x_sample.shape[0]
    c_all = jnp.concatenate([c_prompt, c_sample], axis=0)
    c_all = jnp.pad(c_all, ((0, (-c_all.shape[0]) % 8), (0, 0)))
    mods = _mods(c_all, ada_w, ada_b[:, None, :], 1536).reshape(DEPTH, -1, 6, D_MODEL)
    kv_mods = _mods(c_all, kv_ada_w[None], kv_ada_b[None, None, :], 1024).reshape(-1, 2, D_MODEL)

    conv0 = jnp.zeros((N_A_LAYERS, bp, CONV_WIDTH - 1, CONV_DIM), F32)
    ssm0 = jnp.zeros((N_A_LAYERS, bp, SSM_HEADS, SSM_HEAD_DIM, SSM_STATE), F32)
    y_p, conv_p, ssm_p, k_p, v_p = _trunk(x_prompt, mods[:, :bp], kv_mods[:bp], conv0, ssm0, None, None,
                                          p, w, PROMPT_CFG)
    y_s, conv_s, ssm_s, k_s, v_s = _trunk(x_sample, mods[:, bp:bp + bs], kv_mods[bp:bp + bs], state_conv,
                                          state_ssm, cache_k, cache_v, p, w, SAMPLE_CFG)
    return (y_p, y_s, conv_p, ssm_p, k_p, v_p, conv_s, ssm_s, k_s, v_s)
```

```python
import functools
import math

import jax
import jax.numpy as jnp
from jax import lax
from jax.experimental import pallas as pl
from jax.experimental.pallas import tpu as pltpu

F32 = jnp.float32
BF16 = jnp.bfloat16
HIGHEST = lax.Precision.HIGHEST

D_MODEL = 1024
DEPTH = 4
N_A_LAYERS = 2
EPS = 1e-6
CHUNK = 64
D_INNER = 2048
SSM_HEADS = 32
SSM_HEAD_DIM = 64
SSM_GROUPS = 4
SSM_STATE = 128
GROUP_WIDTH = D_INNER // SSM_GROUPS
CONV_WIDTH = 4
CONV_DIM = D_INNER + 2 * SSM_GROUPS * SSM_STATE
DT_PAD = 128
IN_PROJ_PAD = D_INNER + CONV_DIM + DT_PAD
DIFF_HEADS = 8
DIFF_HEAD_DIM = 64
DIFF_V_DIM = 128
N_GROUPS = 4
EXPERTS_PER_GROUP = 8
N_EXPERTS = 32
D_EXPERT = 512

LANES = 128
ROW_SLABS = D_MODEL // LANES
NEG_BIG = -1e30
FIXED_SHIFT_LIMIT = 40.0
VMEM_LIMIT = 56 * 1024 * 1024


def _params(semantics, vmem=VMEM_LIMIT):
    return pltpu.CompilerParams(dimension_semantics=semantics, vmem_limit_bytes=vmem)


def _silu(x):
    return x * (1.0 / (1.0 + jnp.exp(-x)))


def _softplus(x):
    return jnp.maximum(x, 0.0) + jnp.log1p(jnp.exp(-jnp.abs(x)))


def _mod_norm(h, gain, shift, scale):
    ms = jnp.mean(h * h, axis=-1, keepdims=True)
    return (h * lax.rsqrt(ms + EPS) * gain) * (1.0 + scale) + shift


def _store_row_slabs(ref, x):
    for s in range(ROW_SLABS):
        ref[:, s, :] = x[:, s * LANES:(s + 1) * LANES]


def _load_row_slabs(ref):
    return jnp.concatenate([ref[:, s, :] for s in range(ROW_SLABS)], axis=1)


def _seg_rms64(x, g, n_cols):
    lane = lax.broadcasted_iota(jnp.int32, (x.shape[0], LANES), 1)
    low = lane < 64
    outs = []
    for cb in range(n_cols // LANES):
        xc = x[:, cb * LANES:(cb + 1) * LANES]
        sq = xc * xc
        s_lo = jnp.sum(jnp.where(low, sq, 0.0), axis=-1, keepdims=True)
        s_hi = jnp.sum(jnp.where(low, 0.0, sq), axis=-1, keepdims=True)
        r = jnp.where(low, lax.rsqrt(s_lo * (1.0 / 64) + EPS), lax.rsqrt(s_hi * (1.0 / 64) + EPS))
        outs.append(xc * r * g[:, cb * LANES:(cb + 1) * LANES])
    if n_cols < x.shape[1]:
        outs.append(x[:, n_cols:])
    return jnp.concatenate(outs, axis=1)


def _mods_kernel(c_ref, w_ref, b_ref, o_ref):
    s = _silu(c_ref[...])
    o_ref[0] = jnp.dot(s.astype(BF16), w_ref[0].astype(BF16), preferred_element_type=F32) + b_ref[0]


def _mods(c_all, w, b, tn):
    nl, _, n = w.shape
    m = c_all.shape[0]
    return pl.pallas_call(
        _mods_kernel,
        grid=(nl, n // tn),
        in_specs=[
            pl.BlockSpec((m, D_MODEL), lambda l, j: (0, 0)),
            pl.BlockSpec((1, D_MODEL, tn), lambda l, j: (l, 0, j)),
            pl.BlockSpec((1, 1, tn), lambda l, j: (l, 0, j)),
        ],
        out_specs=pl.BlockSpec((1, m, tn), lambda l, j: (l, 0, j)),
        out_shape=jax.ShapeDtypeStruct((nl, m, n), F32),
        compiler_params=_params(("parallel", "parallel")),
        name="adaln_mods",
    )(c_all, w, b)


def _expand_heads(v, e2_ref):
    hi = v.astype(BF16)
    lo = (v - hi.astype(F32)).astype(BF16)
    return jnp.dot(jnp.concatenate([hi, lo], axis=1), e2_ref[...], preferred_element_type=F32)


def _mamba_kernel(h_ref, mod_ref, ng_ref, w_ref, cw_ref, cb_ref, dtb_ref, alog_ref, dsk_ref, sng_ref,
                  wo_ref, cprev_ref, sprev_ref, e2_ref,
                  hout_ref, cnew_ref, snew_ref,
                  xbc_scr, st_scr, xd_scr, y_scr, *, lt, lc, nc):
    c = pl.program_id(1)

    @pl.when(c == 0)
    def _():
        xbc_scr[...] = cprev_ref[0]
        st_scr[...] = sprev_ref[0].T

    h = h_ref[0]
    mod = mod_ref[0]
    u = _mod_norm(h, ng_ref[...], mod[0:1], mod[1:2])
    proj = jnp.dot(u.astype(BF16), w_ref[...], preferred_element_type=F32)
    z = proj[:, :D_INNER]
    xbc_raw = proj[:, D_INNER:D_INNER + CONV_DIM]
    dt_raw = proj[:, D_INNER + CONV_DIM:]

    cw = cw_ref[...]
    prev8 = xbc_scr[...]
    row8 = lax.broadcasted_iota(jnp.int32, (8, CONV_DIM), 0)
    conv = cb_ref[...] + cw[CONV_WIDTH - 1:CONV_WIDTH] * xbc_raw
    for s in range(1, CONV_WIDTH):
        rolled = pltpu.roll(xbc_raw, s, 0)
        first8 = jnp.where(row8 < s, pltpu.roll(prev8, s, 0), rolled[0:8])
        tap = CONV_WIDTH - 1 - s
        conv = conv + cw[tap:tap + 1] * jnp.concatenate([first8, rolled[8:]], axis=0)
    tail = xbc_raw[lt - 8:lt]
    cnew_ref[0] = tail
    xbc_scr[...] = tail
    xbc = _silu(conv)
    xs = xbc[:, :D_INNER]
    bm = xbc[:, D_INNER:D_INNER + GROUP_WIDTH]
    cm = xbc[:, D_INNER + GROUP_WIDTH:].astype(BF16)

    dt = _softplus(dt_raw + dtb_ref[...])
    da = dt * (-jnp.exp(alog_ref[...]))
    xd_scr[...] = xs * _expand_heads(dt, e2_ref)

    row_i = lax.broadcasted_iota(jnp.int32, (lc, lc), 0)
    col_j = lax.broadcasted_iota(jnp.int32, (lc, lc), 1)
    tril = col_j <= row_i
    trilf = tril.astype(F32)
    low = lax.broadcasted_iota(jnp.int32, (lc, LANES), 1) < SSM_HEAD_DIM

    for s in range(lt // lc):
        r0 = s * lc
        acum = jnp.dot(trilf, da[r0:r0 + lc], precision=HIGHEST, preferred_element_type=F32)
        acum_t = acum.T
        acum_e = _expand_heads(acum, e2_ref)
        ea = jnp.exp(acum_e)
        dend = jnp.exp(acum_e[lc - 1:lc] - acum_e)
        xd_s = xd_scr[r0:r0 + lc, :]
        xdb = xd_s.astype(BF16)
        xdw = (xd_s * dend).astype(BF16)
        for g in range(SSM_GROUPS):
            gl = slice(g * GROUP_WIDTH, (g + 1) * GROUP_WIDTH)
            b_g = bm[r0:r0 + lc, g * SSM_STATE:(g + 1) * SSM_STATE]
            c_g = cm[r0:r0 + lc, g * SSM_STATE:(g + 1) * SSM_STATE]
            cbm = lax.dot_general(c_g, b_g.astype(BF16), (((1,), (1,)), ((), ())),
                                  preferred_element_type=F32)
            s_g = st_scr[:, gl]
            yoff = jnp.dot(c_g, s_g.astype(BF16), preferred_element_type=F32) * ea[:, gl]
            for j in range(GROUP_WIDTH // LANES):
                pair = g * (GROUP_WIDTH // LANES) + j
                pl_ = slice(pair * LANES, (pair + 1) * LANES)
                ys = []
                for hh in (2 * pair, 2 * pair + 1):
                    seg = acum[:, hh:hh + 1] - acum_t[hh:hh + 1, :]
                    dec = jnp.exp(jnp.where(tril, seg, NEG_BIG))
                    ys.append(jnp.dot((cbm * dec).astype(BF16), xdb[:, pl_], preferred_element_type=F32))
                y_scr[r0:r0 + lc, pl_] = jnp.where(low, ys[0], ys[1]) + yoff[:, j * LANES:(j + 1) * LANES]
            upd = jnp.dot(b_g.T.astype(BF16), xdw[:, gl], preferred_element_type=F32)
            st_scr[:, gl] = s_g * ea[lc - 1:lc, gl] + upd

    y = (y_scr[...] + dsk_ref[...] * xs) * _silu(z)
    parts = []
    for g in range(SSM_GROUPS):
        gl = slice(g * GROUP_WIDTH, (g + 1) * GROUP_WIDTH)
        yg = y[:, gl]
        ms = jnp.mean(yg * yg, axis=-1, keepdims=True)
        parts.append(yg * lax.rsqrt(ms + EPS) * sng_ref[:, gl])
    yn = jnp.concatenate(parts, axis=1).astype(BF16)
    out = jnp.dot(yn, wo_ref[...], preferred_element_type=F32)
    hout_ref[0] = h + mod[2:3] * out

    @pl.when(c == nc - 1)
    def _():
        snew_ref[0] = st_scr[...].T


def _mamba_layer(h, mod, ng, w_in, cw, cb, dtb, alog, dsk, sng, wo, cprev, sprev, e2, lt, lc):
    bsz, seq, _ = h.shape
    nc = seq // lt
    const2 = lambda b, c: (0, 0)
    perb = lambda b, c: (b, 0, 0)
    kern = functools.partial(_mamba_kernel, lt=lt, lc=lc, nc=nc)
    return pl.pallas_call(
        kern,
        grid=(bsz, nc),
        in_specs=[
            pl.BlockSpec((1, lt, D_MODEL), lambda b, c: (b, c, 0)),
            pl.BlockSpec((1, 6, D_MODEL), perb),
            pl.BlockSpec((1, D_MODEL), const2),
            pl.BlockSpec((D_MODEL, IN_PROJ_PAD), const2),
            pl.BlockSpec((CONV_WIDTH, CONV_DIM), const2),
            pl.BlockSpec((1, CONV_DIM), const2),
            pl.BlockSpec((1, DT_PAD), const2),
            pl.BlockSpec((1, DT_PAD), const2),
            pl.BlockSpec((1, D_INNER), const2),
            pl.BlockSpec((1, D_INNER), const2),
            pl.BlockSpec((D_INNER, D_MODEL), const2),
            pl.BlockSpec((1, 8, CONV_DIM), perb),
            pl.BlockSpec((1, D_INNER, SSM_STATE), perb),
            pl.BlockSpec((2 * DT_PAD, D_INNER), const2),
        ],
        out_specs=[
            pl.BlockSpec((1, lt, D_MODEL), lambda b, c: (b, c, 0)),
            pl.BlockSpec((1, 8, CONV_DIM), perb),
            pl.BlockSpec((1, D_INNER, SSM_STATE), perb),
        ],
        out_shape=[
            jax.ShapeDtypeStruct((bsz, seq, D_MODEL), F32),
            jax.ShapeDtypeStruct((bsz, 8, CONV_DIM), F32),
            jax.ShapeDtypeStruct((bsz, D_INNER, SSM_STATE), F32),
        ],
        scratch_shapes=[
            pltpu.VMEM((8, CONV_DIM), F32),
            pltpu.VMEM((SSM_STATE, D_INNER), F32),
            pltpu.VMEM((lt, D_INNER), F32),
            pltpu.VMEM((lt, D_INNER), F32),
        ],
        compiler_params=_params(("parallel", "arbitrary")),
        name="mamba_layer",
    )(h, mod, ng, w_in, cw, cb, dtb, alog, dsk, sng, wo, cprev, sprev, e2)


def _normproj_kernel(h_ref, mod_ref, ng_ref, w_ref, sg_ref, *out_refs, n_norm, want_f32):
    mod = mod_ref[0]
    u = _mod_norm(h_ref[0], ng_ref[...], mod[0:1], mod[1:2])
    y = jnp.dot(u.astype(BF16), w_ref[...], preferred_element_type=F32)
    y = _seg_rms64(y, sg_ref[...], n_norm)
    if want_f32:
        out_refs[0][0] = y
    out_refs[-1][0] = y.astype(BF16)


def _normproj(h, mod2, ng, w, seg_g, tl, want_f32, name):
    bsz, seq, _ = h.shape
    nout = w.shape[1]
    n_norm = seg_g.shape[1]
    tok = lambda b, t: (b, t, 0)
    const2 = lambda b, t: (0, 0)
    out_specs = [pl.BlockSpec((1, tl, nout), tok)]
    out_shape = [jax.ShapeDtypeStruct((bsz, seq, nout), BF16)]
    if want_f32:
        out_specs = [pl.BlockSpec((1, tl, nout), tok)] + out_specs
        out_shape = [jax.ShapeDtypeStruct((bsz, seq, nout), F32)] + out_shape
    return pl.pallas_call(
        functools.partial(_normproj_kernel, n_norm=n_norm, want_f32=want_f32),
        grid=(bsz, seq // tl),
        in_specs=[
            pl.BlockSpec((1, tl, D_MODEL), tok),
            pl.BlockSpec((1, 2, D_MODEL), lambda b, t: (b, 0, 0)),
            pl.BlockSpec((1, D_MODEL), const2),
            pl.BlockSpec((D_MODEL, nout), const2),
            pl.BlockSpec((1, n_norm), const2),
        ],
        out_specs=out_specs,
        out_shape=out_shape,
        compiler_params=_params(("parallel", "parallel")),
        name=name,
    )(h, mod2, ng, w, seg_g)


def _attn_kernel(slope_ref, q_ref, k_ref, v_ref, lam_ref, sg_ref, o_ref,
                 m1, l1, a1, m2, l2, a2, *, tq, tk, nk, pos0, lam_init):
    hd = pl.program_id(1)
    qi = pl.program_id(2)
    ki = pl.program_id(3)

    @pl.when(ki == 0)
    def _():
        for m, l, a in ((m1, l1, a1), (m2, l2, a2)):
            m[...] = jnp.full(m.shape, NEG_BIG, F32)
            l[...] = jnp.zeros(l.shape, F32)
            a[...] = jnp.zeros(a.shape, F32)

    q_first = pos0 + qi * tq
    last_q_chunk = (q_first + tq - 1) // CHUNK
    first_k_chunk = (ki * tk) // CHUNK

    @pl.when(first_k_chunk <= last_q_chunk)
    def _():
        q = q_ref[0]
        k = k_ref[0]
        v = v_ref[0]
        lane = lax.broadcasted_iota(jnp.int32, q.shape, 1)
        zero = jnp.zeros_like(q)
        qa = jnp.where(lane < DIFF_HEAD_DIM, q, zero)
        qb = jnp.where(lane < DIFF_HEAD_DIM, zero, q)
        qpos = q_first + lax.broadcasted_iota(jnp.int32, (tq, tk), 0)
        kpos = ki * tk + lax.broadcasted_iota(jnp.int32, (tq, tk), 1)
        visible = (kpos // CHUNK) <= (qpos // CHUNK)
        dist = jnp.abs(qpos - kpos).astype(F32)
        bias = jnp.where(visible, -slope_ref[hd] * dist, NEG_BIG)
        scale = DIFF_HEAD_DIM ** -0.5
        nt = (((1,), (1,)), ((), ()))
        for qq, m, l, a in ((qa, m1, l1, a1), (qb, m2, l2, a2)):
            s = lax.dot_general(qq, k, nt, preferred_element_type=F32) * scale + bias
            m_old = m[...]
            m_new = jnp.maximum(m_old, jnp.max(s, axis=-1, keepdims=True))
            p = jnp.exp(s - m_new)
            alpha = jnp.exp(m_old - m_new)
            l[...] = alpha * l[...] + jnp.sum(p, axis=-1, keepdims=True)
            a[...] = alpha * a[...] + jnp.dot(p.astype(BF16), v, preferred_element_type=F32)
            m[...] = m_new

    @pl.when(ki == nk - 1)
    def _():
        lp = lam_ref[...]
        lam = (jnp.exp(jnp.sum(lp[0:1] * lp[1:2], axis=-1, keepdims=True))
               - jnp.exp(jnp.sum(lp[2:3] * lp[3:4], axis=-1, keepdims=True)) + lam_init)
        o = a1[...] / l1[...] - lam * (a2[...] / l2[...])
        ms = jnp.mean(o * o, axis=-1, keepdims=True)
        o_ref[0] = (o * lax.rsqrt(ms + EPS) * sg_ref[...] * (1.0 - lam_init)).astype(BF16)


def _diff_attention(q, k, v, slopes, lam_rows, sub_g, tq, tk, pos0, lam_init):
    bsz, lq, _ = q.shape
    lk = k.shape[1]
    nq, nk = lq // tq, lk // tk

    def kv_map(b, h, i, j, slope_ref):
        last_visible = ((pos0 + i * tq + tq - 1) // CHUNK * CHUNK + CHUNK - 1) // tk
        return (b, jnp.minimum(j, last_visible), h)

    grid_spec = pltpu.PrefetchScalarGridSpec(
        num_scalar_prefetch=1,
        grid=(bsz, DIFF_HEADS, nq, nk),
        in_specs=[
            pl.BlockSpec((1, tq, LANES), lambda b, h, i, j, s: (b, i, h)),
            pl.BlockSpec((1, tk, LANES), kv_map),
            pl.BlockSpec((1, tk, LANES), kv_map),
            pl.BlockSpec((8, LANES), lambda b, h, i, j, s: (0, 0)),
            pl.BlockSpec((1, LANES), lambda b, h, i, j, s: (0, 0)),
        ],
        out_specs=pl.BlockSpec((1, tq, LANES), lambda b, h, i, j, s: (b, i, h)),
        scratch_shapes=[
            pltpu.VMEM((tq, 1), F32), pltpu.VMEM((tq, 1), F32), pltpu.VMEM((tq, LANES), F32),
            pltpu.VMEM((tq, 1), F32), pltpu.VMEM((tq, 1), F32), pltpu.VMEM((tq, LANES), F32),
        ],
    )
    return pl.pallas_call(
        functools.partial(_attn_kernel, tq=tq, tk=tk, nk=nk, pos0=pos0, lam_init=lam_init),
        grid_spec=grid_spec,
        out_shape=jax.ShapeDtypeStruct((bsz, lq, DIFF_HEADS * DIFF_V_DIM), BF16),
        compiler_params=_params(("parallel", "parallel", "parallel", "arbitrary")),
        name="diff_attention",
    )(slopes, q, k, v, lam_rows, sub_g)


def _attn_fixed_kernel(sc_ref, q_ref, kt_ref, v_ref, lam_ref, sg_ref, o_ref, acc1, acc2, *, t, lam_init):
    hd = pl.program_id(1)
    qi = pl.program_id(2)
    slope = sc_ref[hd]
    shift = sc_ref[DIFF_HEADS]

    lane = lax.broadcasted_iota(jnp.int32, (t, LANES), 1)
    qoff = lax.broadcasted_iota(jnp.int32, (t, LANES), 0)
    q_hi = (qoff >> 8).astype(F32) * 256.0
    q_lo = (qoff & 255).astype(F32)
    aug_q = jnp.where(lane == 0, -slope * q_hi, jnp.where(lane == 1, -slope * q_lo, jnp.where(
        (lane == 2) | (lane == 3), 1.0, jnp.where(lane == 4, -shift, 0.0)))).astype(BF16)
    qs = q_ref[0] * jnp.asarray(DIFF_HEAD_DIM ** -0.5, BF16)
    zero = jnp.zeros_like(qs)
    qa = jnp.concatenate([jnp.where(lane < DIFF_HEAD_DIM, qs, zero), aug_q], axis=1)
    qb = jnp.concatenate([jnp.where(lane < DIFF_HEAD_DIM, zero, qs), aug_q], axis=1)
    acc1[...] = jnp.zeros(acc1.shape, F32)
    acc2[...] = jnp.zeros(acc2.shape, F32)
    ones = jnp.ones((t, LANES), BF16)
    aug_row = lax.broadcasted_iota(jnp.int32, (16, t), 0)
    koff = lax.broadcasted_iota(jnp.int32, (16, t), 1)
    pad_rows = jnp.zeros((LANES - 16, t), BF16)

    def accumulate(kb, aug_k, bias):
        k0 = pl.multiple_of(kb * t, t)
        kfull = jnp.concatenate([kt_ref[0, 0, kb], aug_k.astype(BF16), pad_rows], axis=0)
        v2 = jnp.concatenate([v_ref[0, pl.ds(k0, t), :], ones], axis=1)
        for qq, acc in ((qa, acc1), (qb, acc2)):
            s = jnp.dot(qq, kfull, preferred_element_type=F32)
            if bias is not None:
                s = s + bias
            acc[...] += jnp.dot(jnp.exp(s).astype(BF16), v2, preferred_element_type=F32)

    def below_diagonal(kb, carry):
        rel = (kb - qi) * t + koff
        k_hi = (rel >> 8).astype(F32) * 256.0
        k_lo = (rel & 255).astype(F32)
        aug_k = jnp.where(aug_row <= 1, 1.0, jnp.where(aug_row == 2, slope * k_hi, jnp.where(
            aug_row == 3, slope * k_lo, jnp.where(aug_row == 4, 1.0, 0.0))))
        accumulate(kb, aug_k, None)
        return carry

    lax.fori_loop(0, qi, below_diagonal, 0)

    qo = lax.broadcasted_iota(jnp.int32, (t, t), 0)
    ko = lax.broadcasted_iota(jnp.int32, (t, t), 1)
    visible = (ko // CHUNK) <= (qo // CHUNK)
    bias = jnp.where(visible, -slope * jnp.abs(qo - ko).astype(F32), NEG_BIG)
    accumulate(qi, jnp.where(aug_row == 4, 1.0, 0.0), bias)

    lp = lam_ref[...]
    lam = (jnp.exp(jnp.sum(lp[0:1] * lp[1:2], axis=-1, keepdims=True))
           - jnp.exp(jnp.sum(lp[2:3] * lp[3:4], axis=-1, keepdims=True)) + lam_init)
    a1 = acc1[...]
    a2 = acc2[...]
    o = a1[:, :LANES] / a1[:, LANES:] - lam * (a2[:, :LANES] / a2[:, LANES:])
    ms = jnp.mean(o * o, axis=-1, keepdims=True)
    o_ref[0] = (o * lax.rsqrt(ms + EPS) * sg_ref[...] * (1.0 - lam_init)).astype(BF16)


def _diff_attention_fixed(q, k, v, slopes_shift, lam_rows, sub_g, t, lam_init):
    bsz, seq, _ = q.shape
    nb = seq // t
    kt = k.reshape(bsz, nb, t, DIFF_HEADS, LANES).transpose(0, 3, 1, 4, 2)
    grid_spec = pltpu.PrefetchScalarGridSpec(
        num_scalar_prefetch=1,
        grid=(bsz, DIFF_HEADS, nb),
        in_specs=[
            pl.BlockSpec((1, t, LANES), lambda b, h, i, s: (b, i, h)),
            pl.BlockSpec((1, 1, nb, LANES, t), lambda b, h, i, s: (b, h, 0, 0, 0)),
            pl.BlockSpec((1, seq, LANES), lambda b, h, i, s: (b, 0, h)),
            pl.BlockSpec((8, LANES), lambda b, h, i, s: (0, 0)),
            pl.BlockSpec((1, LANES), lambda b, h, i, s: (0, 0)),
        ],
        out_specs=pl.BlockSpec((1, t, LANES), lambda b, h, i, s: (b, i, h)),
        scratch_shapes=[pltpu.VMEM((t, 2 * LANES), F32), pltpu.VMEM((t, 2 * LANES), F32)],
    )
    return pl.pallas_call(
        functools.partial(_attn_fixed_kernel, t=t, lam_init=lam_init),
        grid_spec=grid_spec,
        out_shape=jax.ShapeDtypeStruct((bsz, seq, DIFF_HEADS * DIFF_V_DIM), BF16),
        compiler_params=_params(("parallel", "parallel", "arbitrary")),
        name="diff_attention_fixed",
    )(slopes_shift, q, kt, v, lam_rows, sub_g)


def _proj_res_kernel(x_ref, w_ref, h_ref, gate_ref, o_ref):
    y = jnp.dot(x_ref[0], w_ref[...], preferred_element_type=F32)
    o_ref[0] = h_ref[0] + gate_ref[0] * y


def _proj_res(x, w, h, gate, tl):
    bsz, seq, kdim = x.shape
    tok = lambda b, t: (b, t, 0)
    return pl.pallas_call(
        _proj_res_kernel,
        grid=(bsz, seq // tl),
        in_specs=[
            pl.BlockSpec((1, tl, kdim), tok),
            pl.BlockSpec((kdim, D_MODEL), lambda b, t: (0, 0)),
            pl.BlockSpec((1, tl, D_MODEL), tok),
            pl.BlockSpec((1, 1, D_MODEL), lambda b, t: (b, 0, 0)),
        ],
        out_specs=pl.BlockSpec((1, tl, D_MODEL), tok),
        out_shape=jax.ShapeDtypeStruct((bsz, seq, D_MODEL), F32),
        compiler_params=_params(("parallel", "parallel")),
        name="attn_out_proj",
    )(x, w, h, gate)


def _moe_route_kernel(h_ref, mod_ref, ng_ref, wr_ref, br_ref, u_ref, route_ref):
    mod = mod_ref[0]
    u = _mod_norm(h_ref[0], ng_ref[...], mod[0:1], mod[1:2])
    _store_row_slabs(u_ref.at[0], u)
    logits = jnp.dot(u.astype(BF16), wr_ref[...], preferred_element_type=F32) + br_ref[...]
    gl = logits[:, :LANES]
    el = logits[:, LANES:]
    lane = lax.broadcasted_iota(jnp.int32, gl.shape, 1)
    lane_f = lane.astype(F32)
    none = float(LANES)

    gmask = lane < N_GROUPS
    glm = jnp.where(gmask, gl, NEG_BIG)
    gmax = jnp.max(glm, axis=-1, keepdims=True)
    g_idx = jnp.min(jnp.where(glm == gmax, lane_f, none), axis=-1, keepdims=True)
    g_w = 1.0 / jnp.sum(jnp.where(gmask, jnp.exp(glm - gmax), 0.0), axis=-1, keepdims=True)

    lo = g_idx * EXPERTS_PER_GROUP
    emask = (lane_f >= lo) & (lane_f < lo + EXPERTS_PER_GROUP)
    elm = jnp.where(emask, el, NEG_BIG)
    emax = jnp.max(elm, axis=-1, keepdims=True)
    ee = jnp.where(emask, jnp.exp(elm - emax), 0.0)
    ep = ee / jnp.sum(ee, axis=-1, keepdims=True)
    epm = jnp.where(emask, ep, -1.0)
    p1 = jnp.max(epm, axis=-1, keepdims=True)
    i1 = jnp.min(jnp.where(epm == p1, lane_f, none), axis=-1, keepdims=True)
    epm2 = jnp.where(lane_f == i1, -1.0, epm)
    p2 = jnp.max(epm2, axis=-1, keepdims=True)
    i2 = jnp.min(jnp.where(epm2 == p2, lane_f, none), axis=-1, keepdims=True)
    denom = p1 + p2
    route_ref[0] = jnp.where(lane == 0, i1, jnp.where(lane == 1, i2, jnp.where(
        lane == 2, p1 / denom * g_w, jnp.where(lane == 3, p2 / denom * g_w, 0.0))))


def _moe_route(h, mod2, ng, wr, br, tl):
    bsz, seq, _ = h.shape
    tok = lambda b, t: (b, t, 0)
    const2 = lambda b, t: (0, 0)
    return pl.pallas_call(
        _moe_route_kernel,
        grid=(bsz, seq // tl),
        in_specs=[
            pl.BlockSpec((1, tl, D_MODEL), tok),
            pl.BlockSpec((1, 2, D_MODEL), lambda b, t: (b, 0, 0)),
            pl.BlockSpec((1, D_MODEL), const2),
            pl.BlockSpec((D_MODEL, 2 * LANES), const2),
            pl.BlockSpec((1, 2 * LANES), const2),
        ],
        out_specs=[pl.BlockSpec((1, tl, ROW_SLABS, LANES), lambda b, t: (b, t, 0, 0)),
                   pl.BlockSpec((1, tl, LANES), tok)],
        out_shape=[jax.ShapeDtypeStruct((bsz, seq, ROW_SLABS, LANES), F32),
                   jax.ShapeDtypeStruct((bsz, seq, LANES), F32)],
        compiler_params=_params(("parallel", "parallel")),
        name="moe_route",
    )(h, mod2, ng, wr, br)


def _moe_schedule(route, tm):
    t = route.shape[0]
    a = 2 * t
    n_tiles = a // tm + N_EXPERTS
    ea = route[:, :2].astype(jnp.int32).reshape(a)
    _, order = lax.sort((ea, jnp.arange(a, dtype=jnp.int32)), num_keys=1, is_stable=True)
    counts = jnp.sum((ea[:, None] == jnp.arange(N_EXPERTS, dtype=jnp.int32)[None, :]).astype(jnp.int32), axis=0)
    padded = (counts + tm - 1) // tm * tm
    pad_end = jnp.cumsum(padded)
    pad_start = pad_end - padded
    start = jnp.cumsum(counts) - counts
    tile_first = jnp.arange(n_tiles, dtype=jnp.int32) * tm
    tile_e = jnp.minimum(jnp.sum((tile_first[:, None] >= pad_end[None, :]).astype(jnp.int32), axis=1),
                         N_EXPERTS - 1)
    tile_valid = (tile_first < pad_end[-1]).astype(jnp.int32)
    slot = jnp.arange(n_tiles * tm, dtype=jnp.int32)
    e_s = jnp.repeat(tile_e, tm)
    local = slot - pad_start[e_s]
    valid = local < counts[e_s]
    asg = order[jnp.clip(start[e_s] + local, 0, a - 1)]
    tok = jnp.where(valid, asg // 2, 0)
    dump = 2 * t + ((slot // tm) % 2) * tm + slot % tm
    dest = jnp.where(valid, (asg % 2) * t + asg // 2, dump)
    return tile_e, tile_valid, tok.reshape(n_tiles, 1, tm), dest.reshape(n_tiles, 1, tm)


def _moe_ffn_kernel(te_ref, tv_ref, tok_ref, tokn_ref, dest_ref, w1_ref, w3_ref, w2_ref, u_hbm, y_hbm,
                    xbuf, ybuf, w1b, w3b, w2b, gsem, ssem, *, tm, n_tiles):
    i = pl.program_id(0)
    slot = lax.rem(i, 2)

    def gather_rows(idx_ref, s):
        def body(r, c):
            pltpu.make_async_copy(u_hbm.at[pl.ds(idx_ref[0, 0, r], 1)], xbuf.at[s, pl.ds(r, 1)],
                                  gsem.at[s]).start()
            return c
        lax.fori_loop(0, tm, body, 0, unroll=8)

    def wait_gather(s):
        pltpu.make_async_copy(u_hbm.at[pl.ds(0, tm)], xbuf.at[s], gsem.at[s]).wait()

    def wait_scatter(s):
        pltpu.make_async_copy(ybuf.at[s], y_hbm.at[pl.ds(0, tm)], ssem.at[s]).wait()

    @pl.when(i == 0)
    def _():
        ybuf[...] = jnp.zeros(ybuf.shape, F32)
        n_real = y_hbm.shape[0] - 2 * tm
        for s in range(2):
            fill = pltpu.make_async_copy(ybuf.at[s], y_hbm.at[pl.ds(n_real + s * tm, tm)], ssem.at[s])
            fill.start()
            fill.wait()

    @pl.when((i == 0) & (tv_ref[0] == 1))
    def _():
        gather_rows(tok_ref, 0)

    @pl.when((i + 1 < n_tiles) & (tv_ref[jnp.minimum(i + 1, n_tiles - 1)] == 1))
    def _():
        gather_rows(tokn_ref, 1 - slot)

    @pl.when((i >= 2) & (tv_ref[jnp.maximum(i - 2, 0)] == 1))
    def _():
        wait_scatter(slot)

    @pl.when(tv_ref[i] == 1)
    def _():
        @pl.when((i == 0) | (te_ref[i] != te_ref[jnp.maximum(i - 1, 0)]))
        def _():
            w1b[...] = w1_ref[0, 0].astype(BF16)
            w3b[...] = w3_ref[0, 0].astype(BF16)
            w2b[...] = w2_ref[0, 0].astype(BF16)

        wait_gather(slot)
        x = _load_row_slabs(xbuf.at[slot]).astype(BF16)
        hid = _silu(jnp.dot(x, w1b[...], preferred_element_type=F32)) * jnp.dot(
            x, w3b[...], preferred_element_type=F32)
        _store_row_slabs(ybuf.at[slot], jnp.dot(hid.astype(BF16), w2b[...], preferred_element_type=F32))

        def body(r, c):
            pltpu.make_async_copy(ybuf.at[slot, pl.ds(r, 1)], y_hbm.at[pl.ds(dest_ref[0, 0, r], 1)],
                                  ssem.at[slot]).start()
            return c
        lax.fori_loop(0, tm, body, 0, unroll=8)

    @pl.when(i == n_tiles - 1)
    def _():
        @pl.when(tv_ref[n_tiles - 2] == 1)
        def _():
            wait_scatter(1 - slot)

        @pl.when(tv_ref[n_tiles - 1] == 1)
        def _():
            wait_scatter(slot)


def _moe_ffn(u2d, tile_e, tile_valid, tok, dest, w1, w3, w2, layer, tm):
    t = u2d.shape[0]
    n_tiles = tok.shape[0]
    smem_tile = lambda f: pl.BlockSpec((1, 1, tm), f, memory_space=pltpu.SMEM)
    wspec = lambda shp: pl.BlockSpec((1,) + shp, lambda i, te, tv: (layer, te[i], 0, 0))
    grid_spec = pltpu.PrefetchScalarGridSpec(
        num_scalar_prefetch=2,
        grid=(n_tiles,),
        in_specs=[
            smem_tile(lambda i, te, tv: (i, 0, 0)),
            smem_tile(lambda i, te, tv: (jnp.minimum(i + 1, n_tiles - 1), 0, 0)),
            smem_tile(lambda i, te, tv: (i, 0, 0)),
            wspec((1, D_MODEL, D_EXPERT)),
            wspec((1, D_MODEL, D_EXPERT)),
            wspec((1, D_EXPERT, D_MODEL)),
            pl.BlockSpec(memory_space=pl.ANY),
        ],
        out_specs=pl.BlockSpec(memory_space=pl.ANY),
        scratch_shapes=[
            pltpu.VMEM((2, tm, ROW_SLABS, LANES), F32),
            pltpu.VMEM((2, tm, ROW_SLABS, LANES), F32),
            pltpu.VMEM((D_MODEL, D_EXPERT), BF16),
            pltpu.VMEM((D_MODEL, D_EXPERT), BF16),
            pltpu.VMEM((D_EXPERT, D_MODEL), BF16),
            pltpu.SemaphoreType.DMA((2,)),
            pltpu.SemaphoreType.DMA((2,)),
        ],
    )
    return pl.pallas_call(
        functools.partial(_moe_ffn_kernel, tm=tm, n_tiles=n_tiles),
        grid_spec=grid_spec,
        out_shape=jax.ShapeDtypeStruct((2 * t + 2 * tm, ROW_SLABS, LANES), F32),
        compiler_params=_params(("arbitrary",)),
        name="moe_ffn",
    )(tile_e, tile_valid, tok, tok, dest, w1, w3, w2, u2d)


def _moe_combine_kernel(h_ref, gate_ref, route_ref, ya_ref, yb_ref, o_ref):
    route = route_ref[0]
    lane = lax.broadcasted_iota(jnp.int32, route.shape, 1)
    wa = jnp.sum(jnp.where(lane == 2, route, 0.0), axis=-1, keepdims=True)
    wb = jnp.sum(jnp.where(lane == 3, route, 0.0), axis=-1, keepdims=True)
    o_ref[0] = h_ref[0] + gate_ref[0] * (wa * _load_row_slabs(ya_ref) + wb * _load_row_slabs(yb_ref))


def _moe_combine(h, gate, route, y2, tl):
    bsz, seq, _ = h.shape
    nt = seq // tl
    tok = lambda b, t: (b, t, 0)
    return pl.pallas_call(
        _moe_combine_kernel,
        grid=(bsz, nt),
        in_specs=[
            pl.BlockSpec((1, tl, D_MODEL), tok),
            pl.BlockSpec((1, 1, D_MODEL), lambda b, t: (b, 0, 0)),
            pl.BlockSpec((1, tl, LANES), tok),
            pl.BlockSpec((tl, ROW_SLABS, LANES), lambda b, t: (b * nt + t, 0, 0)),
            pl.BlockSpec((tl, ROW_SLABS, LANES), lambda b, t: (bsz * nt + b * nt + t, 0, 0)),
        ],
        out_specs=pl.BlockSpec((1, tl, D_MODEL), tok),
        out_shape=jax.ShapeDtypeStruct((bsz, seq, D_MODEL), F32),
        compiler_params=_params(("parallel", "parallel")),
        name="moe_combine",
    )(h, gate, route, y2, y2)


def _alibi_slopes():
    start = 2.0 ** (-8.0 / DIFF_HEADS)
    return jnp.asarray([start ** (i + 1) for i in range(DIFF_HEADS)], F32)


def _head_expand_matrix():
    head_of_lane = jnp.arange(D_INNER) // SSM_HEAD_DIM
    e = (jnp.arange(DT_PAD)[:, None] == head_of_lane[None, :]).astype(BF16)
    return jnp.concatenate([e, e], axis=0)


def _pad_lanes(x, width):
    return jnp.pad(x, [(0, 0)] * (x.ndim - 1) + [(0, width - x.shape[-1])])


def _prep_weights(p):
    w = {}
    w_in = p["ssm_w_in"]
    w["ssm_w_in"] = jnp.concatenate(
        [w_in[..., :D_INNER + CONV_DIM], _pad_lanes(w_in[..., D_INNER + CONV_DIM:], DT_PAD)], axis=-1).astype(BF16)
    w["ssm_dt_bias"] = _pad_lanes(p["ssm_dt_bias"], DT_PAD)[:, None, :]
    w["ssm_a_log"] = _pad_lanes(p["ssm_a_log"], DT_PAD)[:, None, :]
    w["ssm_d"] = jnp.repeat(p["ssm_d"], SSM_HEAD_DIM, axis=-1)[:, None, :]
    w["ssm_w_out"] = p["ssm_w_out"].astype(BF16)
    w["e2"] = _head_expand_matrix()
    w["w_kv"] = p["w_kv"].astype(BF16)
    w["k_norm_g"] = jnp.tile(p["k_norm_g"], 2 * DIFF_HEADS)[None, :]
    w["attn_w_q"] = p["attn_w_q"].astype(BF16)
    w["attn_q_norm_g"] = jnp.tile(p["attn_q_norm_g"], (1, 2 * DIFF_HEADS))[:, None, :]
    lam = jnp.stack([p["attn_lambda_q1"], p["attn_lambda_k1"], p["attn_lambda_q2"], p["attn_lambda_k2"]], axis=1)
    w["lam_rows"] = jnp.pad(lam, ((0, 0), (0, 4), (0, LANES - DIFF_HEAD_DIM)))
    w["attn_w_o"] = p["attn_w_o"].astype(BF16)
    wr = jnp.concatenate([_pad_lanes(p["moe_w_group"], LANES), _pad_lanes(p["moe_w_expert"], LANES)], axis=-1)
    br = jnp.concatenate([_pad_lanes(p["moe_b_group"], LANES), _pad_lanes(p["moe_b_expert"], LANES)], axis=-1)
    w["moe_wr"] = wr.astype(BF16)
    w["moe_br"] = br[:, None, :]
    return w


def _trunk(x, mods, kv_mod, conv_prev, ssm_prev, k_past, v_past, p, w, cfg):
    bsz, seq, _ = x.shape
    tl, lt, lc, tq, tk = cfg["tl"], cfg["lt"], cfg["lc"], cfg["tq"], cfg["tk"]
    pos0 = 0 if k_past is None else k_past.shape[1]
    h = x
    conv_new, ssm_new = [], []
    k_new = v_new = k_all = v_all = None
    for layer in range(DEPTH):
        mod = mods[layer]
        if layer < N_A_LAYERS:
            i = layer
            cprev = jnp.pad(conv_prev[i], ((0, 0), (8 - (CONV_WIDTH - 1), 0), (0, 0)))
            sprev = ssm_prev[i].reshape(bsz, D_INNER, SSM_STATE)
            h, cn, sn = _mamba_layer(
                h, mod, p["norm_g"][layer, 0][None, :], w["ssm_w_in"][i], p["ssm_conv_w"][i],
                p["ssm_conv_b"][i][None, :], w["ssm_dt_bias"][i], w["ssm_a_log"][i], w["ssm_d"][i],
                p["ssm_norm_g"][i][None, :], w["ssm_w_out"][i], cprev, sprev, w["e2"], lt, lc)
            conv_new.append(cn[:, 8 - (CONV_WIDTH - 1):, :])
            ssm_new.append(sn.reshape(bsz, SSM_HEADS, SSM_HEAD_DIM, SSM_STATE))
        else:
            j = layer - N_A_LAYERS
            lam_init = 0.8 - 0.6 * math.exp(-0.3 * layer)
            q = _normproj(h, mod[:, 0:2], p["norm_g"][layer, 0][None, :], w["attn_w_q"][j],
                          w["attn_q_norm_g"][j], tl, False, "attn_q_proj")[0]
            slopes = _alibi_slopes()
            sub_g = p["attn_sub_g"][j][None, :]
            online = functools.partial(_diff_attention, slopes=slopes, lam_rows=w["lam_rows"][j], sub_g=sub_g,
                                       tq=tq, tk=tk, pos0=pos0, lam_init=lam_init)
            if k_past is None and tq == tk:
                bound = (DIFF_HEAD_DIM ** 0.5 * (1.0 + 2.0 ** -6) * jnp.max(jnp.abs(p["attn_q_norm_g"][j]))
                         * jnp.max(jnp.abs(p["k_norm_g"])))
                slopes_shift = jnp.concatenate([slopes, bound[None], jnp.zeros((7,), F32)])
                o = lax.cond(
                    bound <= FIXED_SHIFT_LIMIT,
                    lambda q_, k_, v_: _diff_attention_fixed(q_, k_, v_, slopes_shift, w["lam_rows"][j], sub_g,
                                                             tq, lam_init),
                    lambda q_, k_, v_: online(q_, k_, v_),
                    q, k_all, v_all)
            else:
                o = online(q, k_all, v_all)
            h = _proj_res(o, w["attn_w_o"][j], h, mod[:, 2:3], tl)
        u, route = _moe_route(h, mod[:, 3:5], p["norm_g"][layer, 1][None, :], w["moe_wr"][layer],
                              w["moe_br"][layer], tl)
        tile_e, tile_valid, tok, dest = _moe_schedule(route.reshape(bsz * seq, LANES), cfg["tm"])
        y2 = _moe_ffn(u.reshape(bsz * seq, ROW_SLABS, LANES), tile_e, tile_valid, tok, dest, p["moe_w1"], p["moe_w3"],
                      p["moe_w2"], layer, cfg["tm"])
        h = _moe_combine(h, mod[:, 5:6], route, y2, tl)
        if layer == N_A_LAYERS - 1:
            kv_f32, kv_bf = _normproj(h, kv_mod, p["kv_norm_g"][None, :], w["w_kv"], w["k_norm_g"], tl, True,
                                      "shared_kv")
            nk_cols = DIFF_HEADS * 2 * DIFF_HEAD_DIM
            k_new = kv_f32[..., :nk_cols].reshape(bsz, seq, DIFF_HEADS, 2, DIFF_HEAD_DIM)
            v_new = kv_f32[..., nk_cols:].reshape(bsz, seq, DIFF_HEADS, DIFF_V_DIM)
            k_all, v_all = kv_bf[..., :nk_cols], kv_bf[..., nk_cols:]
            if k_past is not None:
                k_all = jnp.concatenate([k_past.reshape(bsz, pos0, nk_cols).astype(BF16), k_all], axis=1)
                v_all = jnp.concatenate([v_past.reshape(bsz, pos0, nk_cols).astype(BF16), v_all], axis=1)
                pad = (-k_all.shape[1]) % tk
                k_all = jnp.pad(k_all, ((0, 0), (0, pad), (0, 0)))
                v_all = jnp.pad(v_all, ((0, 0), (0, pad), (0, 0)))
    return h, jnp.stack(conv_new), jnp.stack(ssm_new), k_new, v_new


PROMPT_CFG = dict(tl=512, lt=256, lc=64, tq=512, tk=512, tm=256)
SAMPLE_CFG = dict(tl=64, lt=64, lc=64, tq=64, tk=1152, tm=64)


def kernel(x_prompt, x_sample, c_prompt, c_sample, state_conv, state_ssm, cache_k, cache_v, ada_w, ada_b, norm_g, ssm_w_in, ssm_conv_w, ssm_conv_b, ssm_dt_bias, ssm_a_log, ssm_d, ssm_norm_g, ssm_w_out, kv_norm_g, kv_ada_w, kv_ada_b, w_kv, k_norm_g, attn_w_q, attn_q_norm_g, attn_lambda_q1, attn_lambda_k1, attn_lambda_q2, attn_lambda_k2, attn_sub_g, attn_w_o, moe_w_group, moe_b_group, moe_w_expert, moe_b_expert, moe_w1, moe_w3, moe_w2):
    p = dict(norm_g=norm_g, ssm_w_in=ssm_w_in, ssm_conv_w=ssm_conv_w, ssm_conv_b=ssm_conv_b,
             ssm_dt_bias=ssm_dt_bias, ssm_a_log=ssm_a_log, ssm_d=ssm_d, ssm_norm_g=ssm_norm_g,
             ssm_w_out=ssm_w_out, kv_norm_g=kv_norm_g, w_kv=w_kv, k_norm_g=k_norm_g, attn_w_q=attn_w_q,
             attn_q_norm_g=attn_q_norm_g, attn_lambda_q1=attn_lambda_q1, attn_lambda_k1=attn_lambda_k1,
             attn_lambda_q2=attn_lambda_q2, attn_lambda_k2=attn_lambda_k2, attn_sub_g=attn_sub_g,
             attn_w_o=attn_w_o, moe_w_group=moe_w_group, moe_b_group=moe_b_group, moe_w_expert=moe_w_expert,
             moe_b_expert=moe_b_expert, moe_w1=moe_w1, moe_w3=moe_w3, moe_w2=moe_w2)
    w = _prep_weights(p)

    bp, bs = x_prompt.shape[0], x_sample.shape[0]
    c_all = jnp.concatenate([c_prompt, c_sample], axis=0)
    c_all = jnp.pad(c_all, ((0, (-c_all.shape[0]) % 8), (0, 0)))
    mods = _mods(c_all, ada_w, ada_b[:, None, :], 1536).reshape(DEPTH, -1, 6, D_MODEL)
    kv_mods = _mods(c_all, kv_ada_w[None], kv_ada_b[None, None, :], 1024).reshape(-1, 2, D_MODEL)

    conv0 = jnp.zeros((N_A_LAYERS, bp, CONV_WIDTH - 1, CONV_DIM), F32)
    ssm0 = jnp.zeros((N_A_LAYERS, bp, SSM_HEADS, SSM_HEAD_DIM, SSM_STATE), F32)
    y_p, conv_p, ssm_p, k_p, v_p = _trunk(x_prompt, mods[:, :bp], kv_mods[:bp], conv0, ssm0, None, None,
                                          p, w, PROMPT_CFG)
    y_s, conv_s, ssm_s, k_s, v_s = _trunk(x_sample, mods[:, bp:bp + bs], kv_mods[bp:bp + bs], state_conv,
                                          state_ssm, cache_k, cache_v, p, w, SAMPLE_CFG)
    return (y_p, y_s, conv_p, ssm_p, k_p, v_p, conv_s, ssm_s, k_s, v_s)
```

```python
import functools
import math

import jax
import jax.numpy as jnp
from jax import lax
from jax.experimental import pallas as pl
from jax.experimental.pallas import tpu as pltpu

F32 = jnp.float32
BF16 = jnp.bfloat16
HIGHEST = lax.Precision.HIGHEST

D_MODEL = 1024
DEPTH = 4
N_A_LAYERS = 2
EPS = 1e-6
CHUNK = 64
D_INNER = 2048
SSM_HEADS = 32
SSM_HEAD_DIM = 64
SSM_GROUPS = 4
SSM_STATE = 128
GROUP_WIDTH = D_INNER // SSM_GROUPS
CONV_WIDTH = 4
CONV_DIM = D_INNER + 2 * SSM_GROUPS * SSM_STATE
DT_PAD = 128
IN_PROJ_PAD = D_INNER + CONV_DIM + DT_PAD
DIFF_HEADS = 8
DIFF_HEAD_DIM = 64
DIFF_V_DIM = 128
N_GROUPS = 4
EXPERTS_PER_GROUP = 8
N_EXPERTS = 32
D_EXPERT = 512

LANES = 128
ROW_SLABS = D_MODEL // LANES
NEG_BIG = -1e30
FIXED_SHIFT_LIMIT = 40.0
VMEM_LIMIT = 56 * 1024 * 1024


def _params(semantics, vmem=VMEM_LIMIT):
    return pltpu.CompilerParams(dimension_semantics=semantics, vmem_limit_bytes=vmem)


def _silu(x):
    return x * (1.0 / (1.0 + jnp.exp(-x)))


def _softplus(x):
    return jnp.maximum(x, 0.0) + jnp.log1p(jnp.exp(-jnp.abs(x)))


def _mod_norm(h, gain, shift, scale):
    ms = jnp.mean(h * h, axis=-1, keepdims=True)
    return (h * lax.rsqrt(ms + EPS) * gain) * (1.0 + scale) + shift


def _store_row_slabs(ref, x):
    for s in range(ROW_SLABS):
        ref[:, s, :] = x[:, s * LANES:(s + 1) * LANES]


def _load_row_slabs(ref):
    return jnp.concatenate([ref[:, s, :] for s in range(ROW_SLABS)], axis=1)


def _seg_rms64(x, g, n_cols):
    lane = lax.broadcasted_iota(jnp.int32, (x.shape[0], LANES), 1)
    low = lane < 64
    outs = []
    for cb in range(n_cols // LANES):
        xc = x[:, cb * LANES:(cb + 1) * LANES]
        sq = xc * xc
        s_lo = jnp.sum(jnp.where(low, sq, 0.0), axis=-1, keepdims=True)
        s_hi = jnp.sum(jnp.where(low, 0.0, sq), axis=-1, keepdims=True)
        r = jnp.where(low, lax.rsqrt(s_lo * (1.0 / 64) + EPS), lax.rsqrt(s_hi * (1.0 / 64) + EPS))
        outs.append(xc * r * g[:, cb * LANES:(cb + 1) * LANES])
    if n_cols < x.shape[1]:
        outs.append(x[:, n_cols:])
    return jnp.concatenate(outs, axis=1)


def _mods_kernel(c_ref, w_ref, b_ref, o_ref):
    s = _silu(c_ref[...])
    o_ref[0] = jnp.dot(s.astype(BF16), w_ref[0].astype(BF16), preferred_element_type=F32) + b_ref[0]


def _mods(c_all, w, b, tn):
    nl, _, n = w.shape
    m = c_all.shape[0]
    return pl.pallas_call(
        _mods_kernel,
        grid=(nl, n // tn),
        in_specs=[
            pl.BlockSpec((m, D_MODEL), lambda l, j: (0, 0)),
            pl.BlockSpec((1, D_MODEL, tn), lambda l, j: (l, 0, j)),
            pl.BlockSpec((1, 1, tn), lambda l, j: (l, 0, j)),
        ],
        out_specs=pl.BlockSpec((1, m, tn), lambda l, j: (l, 0, j)),
        out_shape=jax.ShapeDtypeStruct((nl, m, n), F32),
        compiler_params=_params(("parallel", "parallel")),
        name="adaln_mods",
    )(c_all, w, b)


def _expand_heads(v, e2_ref):
    hi = v.astype(BF16)
    lo = (v - hi.astype(F32)).astype(BF16)
    return jnp.dot(jnp.concatenate([hi, lo], axis=1), e2_ref[...], preferred_element_type=F32)


def _mamba_kernel(h_ref, mod_ref, ng_ref, w_ref, cw_ref, cb_ref, dtb_ref, alog_ref, dsk_ref, sng_ref,
                  wo_ref, cprev_ref, sprev_ref, e2_ref,
                  hout_ref, cnew_ref, snew_ref,
                  xbc_scr, st_scr, xd_scr, y_scr, *, lt, lc, nc):
    c = pl.program_id(1)

    @pl.when(c == 0)
    def _():
        xbc_scr[...] = cprev_ref[0]
        st_scr[...] = sprev_ref[0].T

    h = h_ref[0]
    mod = mod_ref[0]
    u = _mod_norm(h, ng_ref[...], mod[0:1], mod[1:2])
    proj = jnp.dot(u.astype(BF16), w_ref[...], preferred_element_type=F32)
    z = proj[:, :D_INNER]
    xbc_raw = proj[:, D_INNER:D_INNER + CONV_DIM]
    dt_raw = proj[:, D_INNER + CONV_DIM:]

    cw = cw_ref[...]
    prev8 = xbc_scr[...]
    row8 = lax.broadcasted_iota(jnp.int32, (8, CONV_DIM), 0)
    conv = cb_ref[...] + cw[CONV_WIDTH - 1:CONV_WIDTH] * xbc_raw
    for s in range(1, CONV_WIDTH):
        rolled = pltpu.roll(xbc_raw, s, 0)
        first8 = jnp.where(row8 < s, pltpu.roll(prev8, s, 0), rolled[0:8])
        tap = CONV_WIDTH - 1 - s
        conv = conv + cw[tap:tap + 1] * jnp.concatenate([first8, rolled[8:]], axis=0)
    tail = xbc_raw[lt - 8:lt]
    cnew_ref[0] = tail
    xbc_scr[...] = tail
    xbc = _silu(conv)
    xs = xbc[:, :D_INNER]
    bm = xbc[:, D_INNER:D_INNER + GROUP_WIDTH]
    cm = xbc[:, D_INNER + GROUP_WIDTH:].astype(BF16)

    dt = _softplus(dt_raw + dtb_ref[...])
    da = dt * (-jnp.exp(alog_ref[...]))
    xd_scr[...] = xs * _expand_heads(dt, e2_ref)

    row_i = lax.broadcasted_iota(jnp.int32, (lc, lc), 0)
    col_j = lax.broadcasted_iota(jnp.int32, (lc, lc), 1)
    tril = col_j <= row_i
    trilf = tril.astype(F32)
    low = lax.broadcasted_iota(jnp.int32, (lc, LANES), 1) < SSM_HEAD_DIM

    for s in range(lt // lc):
        r0 = s * lc
        acum = jnp.dot(trilf, da[r0:r0 + lc], precision=HIGHEST, preferred_element_type=F32)
        acum_t = acum.T
        acum_e = _expand_heads(acum, e2_ref)
        ea = jnp.exp(acum_e)
        dend = jnp.exp(acum_e[lc - 1:lc] - acum_e)
        xd_s = xd_scr[r0:r0 + lc, :]
        xdb = xd_s.astype(BF16)
        xdw = (xd_s * dend).astype(BF16)
        for g in range(SSM_GROUPS):
            gl = slice(g * GROUP_WIDTH, (g + 1) * GROUP_WIDTH)
            b_g = bm[r0:r0 + lc, g * SSM_STATE:(g + 1) * SSM_STATE]
            c_g = cm[r0:r0 + lc, g * SSM_STATE:(g + 1) * SSM_STATE]
            cbm = lax.dot_general(c_g, b_g.astype(BF16), (((1,), (1,)), ((), ())),
                                  preferred_element_type=F32)
            s_g = st_scr[:, gl]
            yoff = jnp.dot(c_g, s_g.astype(BF16), preferred_element_type=F32) * ea[:, gl]
            for j in range(GROUP_WIDTH // LANES):
                pair = g * (GROUP_WIDTH // LANES) + j
                pl_ = slice(pair * LANES, (pair + 1) * LANES)
                ys = []
                for hh in (2 * pair, 2 * pair + 1):
                    seg = acum[:, hh:hh + 1] - acum_t[hh:hh + 1, :]
                    dec = jnp.exp(jnp.where(tril, seg, NEG_BIG))
                    ys.append(jnp.dot((cbm * dec).astype(BF16), xdb[:, pl_], preferred_element_type=F32))
                y_scr[r0:r0 + lc, pl_] = jnp.where(low, ys[0], ys[1]) + yoff[:, j * LANES:(j + 1) * LANES]
            upd = jnp.dot(b_g.T.astype(BF16), xdw[:, gl], preferred_element_type=F32)
            st_scr[:, gl] = s_g * ea[lc - 1:lc, gl] + upd

    y = (y_scr[...] + dsk_ref[...] * xs) * _silu(z)
    parts = []
    for g in range(SSM_GROUPS):
        gl = slice(g * GROUP_WIDTH, (g + 1) * GROUP_WIDTH)
        yg = y[:, gl]
        ms = jnp.mean(yg * yg, axis=-1, keepdims=True)
        parts.append(yg * lax.rsqrt(ms + EPS) * sng_ref[:, gl])
    yn = jnp.concatenate(parts, axis=1).astype(BF16)
    out = jnp.dot(yn, wo_ref[...], preferred_element_type=F32)
    hout_ref[0] = h + mod[2:3] * out

    @pl.when(c == nc - 1)
    def _():
        snew_ref[0] = st_scr[...].T


def _mamba_layer(h, mod, ng, w_in, cw, cb, dtb, alog, dsk, sng, wo, cprev, sprev, e2, lt, lc):
    bsz, seq, _ = h.shape
    nc = seq // lt
    const2 = lambda b, c: (0, 0)
    perb = lambda b, c: (b, 0, 0)
    kern = functools.partial(_mamba_kernel, lt=lt, lc=lc, nc=nc)
    return pl.pallas_call(
        kern,
        grid=(bsz, nc),
        in_specs=[
            pl.BlockSpec((1, lt, D_MODEL), lambda b, c: (b, c, 0)),
            pl.BlockSpec((1, 6, D_MODEL), perb),
            pl.BlockSpec((1, D_MODEL), const2),
            pl.BlockSpec((D_MODEL, IN_PROJ_PAD), const2),
            pl.BlockSpec((CONV_WIDTH, CONV_DIM), const2),
            pl.BlockSpec((1, CONV_DIM), const2),
            pl.BlockSpec((1, DT_PAD), const2),
            pl.BlockSpec((1, DT_PAD), const2),
            pl.BlockSpec((1, D_INNER), const2),
            pl.BlockSpec((1, D_INNER), const2),
            pl.BlockSpec((D_INNER, D_MODEL), const2),
            pl.BlockSpec((1, 8, CONV_DIM), perb),
            pl.BlockSpec((1, D_INNER, SSM_STATE), perb),
            pl.BlockSpec((2 * DT_PAD, D_INNER), const2),
        ],
        out_specs=[
            pl.BlockSpec((1, lt, D_MODEL), lambda b, c: (b, c, 0)),
            pl.BlockSpec((1, 8, CONV_DIM), perb),
            pl.BlockSpec((1, D_INNER, SSM_STATE), perb),
        ],
        out_shape=[
            jax.ShapeDtypeStruct((bsz, seq, D_MODEL), F32),
            jax.ShapeDtypeStruct((bsz, 8, CONV_DIM), F32),
            jax.ShapeDtypeStruct((bsz, D_INNER, SSM_STATE), F32),
        ],
        scratch_shapes=[
            pltpu.VMEM((8, CONV_DIM), F32),
            pltpu.VMEM((SSM_STATE, D_INNER), F32),
            pltpu.VMEM((lt, D_INNER), F32),
            pltpu.VMEM((lt, D_INNER), F32),
        ],
        compiler_params=_params(("parallel", "arbitrary")),
        name="mamba_layer",
    )(h, mod, ng, w_in, cw, cb, dtb, alog, dsk, sng, wo, cprev, sprev, e2)


def _normproj_kernel(h_ref, mod_ref, ng_ref, w_ref, sg_ref, *out_refs, n_norm, want_f32):
    mod = mod_ref[0]
    u = _mod_norm(h_ref[0], ng_ref[...], mod[0:1], mod[1:2])
    y = jnp.dot(u.astype(BF16), w_ref[...], preferred_element_type=F32)
    y = _seg_rms64(y, sg_ref[...], n_norm)
    if want_f32:
        out_refs[0][0] = y
    out_refs[-1][0] = y.astype(BF16)


def _normproj(h, mod2, ng, w, seg_g, tl, want_f32, name):
    bsz, seq, _ = h.shape
    nout = w.shape[1]
    n_norm = seg_g.shape[1]
    tok = lambda b, t: (b, t, 0)
    const2 = lambda b, t: (0, 0)
    out_specs = [pl.BlockSpec((1, tl, nout), tok)]
    out_shape = [jax.ShapeDtypeStruct((bsz, seq, nout), BF16)]
    if want_f32:
        out_specs = [pl.BlockSpec((1, tl, nout), tok)] + out_specs
        out_shape = [jax.ShapeDtypeStruct((bsz, seq, nout), F32)] + out_shape
    return pl.pallas_call(
        functools.partial(_normproj_kernel, n_norm=n_norm, want_f32=want_f32),
        grid=(bsz, seq // tl),
        in_specs=[
            pl.BlockSpec((1, tl, D_MODEL), tok),
            pl.BlockSpec((1, 2, D_MODEL), lambda b, t: (b, 0, 0)),
            pl.BlockSpec((1, D_MODEL), const2),
            pl.BlockSpec((D_MODEL, nout), const2),
            pl.BlockSpec((1, n_norm), const2),
        ],
        out_specs=out_specs,
        out_shape=out_shape,
        compiler_params=_params(("parallel", "parallel")),
        name=name,
    )(h, mod2, ng, w, seg_g)


def _attn_kernel(slope_ref, q_ref, k_ref, v_ref, lam_ref, sg_ref, o_ref,
                 m1, l1, a1, m2, l2, a2, *, tq, tk, nk, pos0, lam_init):
    hd = pl.program_id(1)
    qi = pl.program_id(2)
    ki = pl.program_id(3)

    @pl.when(ki == 0)
    def _():
        for m, l, a in ((m1, l1, a1), (m2, l2, a2)):
            m[...] = jnp.full(m.shape, NEG_BIG, F32)
            l[...] = jnp.zeros(l.shape, F32)
            a[...] = jnp.zeros(a.shape, F32)

    q_first = pos0 + qi * tq
    last_q_chunk = (q_first + tq - 1) // CHUNK
    first_k_chunk = (ki * tk) // CHUNK

    @pl.when(first_k_chunk <= last_q_chunk)
    def _():
        q = q_ref[0]
        k = k_ref[0]
        v = v_ref[0]
        lane = lax.broadcasted_iota(jnp.int32, q.shape, 1)
        zero = jnp.zeros_like(q)
        qa = jnp.where(lane < DIFF_HEAD_DIM, q, zero)
        qb = jnp.where(lane < DIFF_HEAD_DIM, zero, q)
        qpos = q_first + lax.broadcasted_iota(jnp.int32, (tq, tk), 0)
        kpos = ki * tk + lax.broadcasted_iota(jnp.int32, (tq, tk), 1)
        visible = (kpos // CHUNK) <= (qpos // CHUNK)
        dist = jnp.abs(qpos - kpos).astype(F32)
        bias = jnp.where(visible, -slope_ref[hd] * dist, NEG_BIG)
        scale = DIFF_HEAD_DIM ** -0.5
        nt = (((1,), (1,)), ((), ()))
        for qq, m, l, a in ((qa, m1, l1, a1), (qb, m2, l2, a2)):
            s = lax.dot_general(qq, k, nt, preferred_element_type=F32) * scale + bias
            m_old = m[...]
            m_new = jnp.maximum(m_old, jnp.max(s, axis=-1, keepdims=True))
            p = jnp.exp(s - m_new)
            alpha = jnp.exp(m_old - m_new)
            l[...] = alpha * l[...] + jnp.sum(p, axis=-1, keepdims=True)
            a[...] = alpha * a[...] + jnp.dot(p.astype(BF16), v, preferred_element_type=F32)
            m[...] = m_new

    @pl.when(ki == nk - 1)
    def _():
        lp = lam_ref[...]
        lam = (jnp.exp(jnp.sum(lp[0:1] * lp[1:2], axis=-1, keepdims=True))
               - jnp.exp(jnp.sum(lp[2:3] * lp[3:4], axis=-1, keepdims=True)) + lam_init)
        o = a1[...] / l1[...] - lam * (a2[...] / l2[...])
        ms = jnp.mean(o * o, axis=-1, keepdims=True)
        o_ref[0] = (o * lax.rsqrt(ms + EPS) * sg_ref[...] * (1.0 - lam_init)).astype(BF16)


def _diff_attention(q, k, v, slopes, lam_rows, sub_g, tq, tk, pos0, lam_init):
    bsz, lq, _ = q.shape
    lk = k.shape[1]
    nq, nk = lq // tq, lk // tk

    def kv_map(b, h, i, j, slope_ref):
        last_visible = ((pos0 + i * tq + tq - 1) // CHUNK * CHUNK + CHUNK - 1) // tk
        return (b, jnp.minimum(j, last_visible), h)

    grid_spec = pltpu.PrefetchScalarGridSpec(
        num_scalar_prefetch=1,
        grid=(bsz, DIFF_HEADS, nq, nk),
        in_specs=[
            pl.BlockSpec((1, tq, LANES), lambda b, h, i, j, s: (b, i, h)),
            pl.BlockSpec((1, tk, LANES), kv_map),
            pl.BlockSpec((1, tk, LANES), kv_map),
            pl.BlockSpec((8, LANES), lambda b, h, i, j, s: (0, 0)),
            pl.BlockSpec((1, LANES), lambda b, h, i, j, s: (0, 0)),
        ],
        out_specs=pl.BlockSpec((1, tq, LANES), lambda b, h, i, j, s: (b, i, h)),
        scratch_shapes=[
            pltpu.VMEM((tq, 1), F32), pltpu.VMEM((tq, 1), F32), pltpu.VMEM((tq, LANES), F32),
            pltpu.VMEM((tq, 1), F32), pltpu.VMEM((tq, 1), F32), pltpu.VMEM((tq, LANES), F32),
        ],
    )
    return pl.pallas_call(
        functools.partial(_attn_kernel, tq=tq, tk=tk, nk=nk, pos0=pos0, lam_init=lam_init),
        grid_spec=grid_spec,
        out_shape=jax.ShapeDtypeStruct((bsz, lq, DIFF_HEADS * DIFF_V_DIM), BF16),
        compiler_params=_params(("parallel", "parallel", "parallel", "arbitrary")),
        name="diff_attention",
    )(slopes, q, k, v, lam_rows, sub_g)


def _attn_fixed_kernel(sc_ref, q_ref, kt_ref, v_ref, lam_ref, sg_ref, o_ref, acc1, acc2, *, t, lam_init):
    hd = pl.program_id(1)
    qi = pl.program_id(2)
    slope = sc_ref[hd]
    shift = sc_ref[DIFF_HEADS]

    lane = lax.broadcasted_iota(jnp.int32, (t, LANES), 1)
    qoff = lax.broadcasted_iota(jnp.int32, (t, LANES), 0)
    q_hi = (qoff >> 8).astype(F32) * 256.0
    q_lo = (qoff & 255).astype(F32)
    aug_q = jnp.where(lane == 0, -slope * q_hi, jnp.where(lane == 1, -slope * q_lo, jnp.where(
        (lane == 2) | (lane == 3), 1.0, jnp.where(lane == 4, -shift, 0.0)))).astype(BF16)
    qs = q_ref[0] * jnp.asarray(DIFF_HEAD_DIM ** -0.5, BF16)
    zero = jnp.zeros_like(qs)
    qa = jnp.concatenate([jnp.where(lane < DIFF_HEAD_DIM, qs, zero), aug_q], axis=1)
    qb = jnp.concatenate([jnp.where(lane < DIFF_HEAD_DIM, zero, qs), aug_q], axis=1)
    acc1[...] = jnp.zeros(acc1.shape, F32)
    acc2[...] = jnp.zeros(acc2.shape, F32)
    ones = jnp.ones((t, LANES), BF16)
    aug_row = lax.broadcasted_iota(jnp.int32, (16, t), 0)
    koff = lax.broadcasted_iota(jnp.int32, (16, t), 1)
    pad_rows = jnp.zeros((LANES - 16, t), BF16)

    def accumulate(kb, aug_k, bias):
        k0 = pl.multiple_of(kb * t, t)
        kfull = jnp.concatenate([kt_ref[0, 0, kb], aug_k.astype(BF16), pad_rows], axis=0)
        v2 = jnp.concatenate([v_ref[0, pl.ds(k0, t), :], ones], axis=1)
        for qq, acc in ((qa, acc1), (qb, acc2)):
            s = jnp.dot(qq, kfull, preferred_element_type=F32)
            if bias is not None:
                s = s + bias
            acc[...] += jnp.dot(jnp.exp(s).astype(BF16), v2, preferred_element_type=F32)

    def below_diagonal(kb, carry):
        rel = (kb - qi) * t + koff
        k_hi = (rel >> 8).astype(F32) * 256.0
        k_lo = (rel & 255).astype(F32)
        aug_k = jnp.where(aug_row <= 1, 1.0, jnp.where(aug_row == 2, slope * k_hi, jnp.where(
            aug_row == 3, slope * k_lo, jnp.where(aug_row == 4, 1.0, 0.0))))
        accumulate(kb, aug_k, None)
        return carry

    lax.fori_loop(0, qi, below_diagonal, 0)

    qo = lax.broadcasted_iota(jnp.int32, (t, t), 0)
    ko = lax.broadcasted_iota(jnp.int32, (t, t), 1)
    visible = (ko // CHUNK) <= (qo // CHUNK)
    bias = jnp.where(visible, -slope * jnp.abs(qo - ko).astype(F32), NEG_BIG)
    accumulate(qi, jnp.where(aug_row == 4, 1.0, 0.0), bias)

    lp = lam_ref[...]
    lam = (jnp.exp(jnp.sum(lp[0:1] * lp[1:2], axis=-1, keepdims=True))
           - jnp.exp(jnp.sum(lp[2:3] * lp[3:4], axis=-1, keepdims=True)) + lam_init)
    a1 = acc1[...]
    a2 = acc2[...]
    o = a1[:, :LANES] / a1[:, LANES:] - lam * (a2[:, :LANES] / a2[:, LANES:])
    ms = jnp.mean(o * o, axis=-1, keepdims=True)
    o_ref[0] = (o * lax.rsqrt(ms + EPS) * sg_ref[...] * (1.0 - lam_init)).astype(BF16)


def _diff_attention_fixed(q, k, v, slopes_shift, lam_rows, sub_g, t, lam_init):
    bsz, seq, _ = q.shape
    nb = seq // t
    kt = k.reshape(bsz, nb, t, DIFF_HEADS, LANES).transpose(0, 3, 1, 4, 2)
    grid_spec = pltpu.PrefetchScalarGridSpec(
        num_scalar_prefetch=1,
        grid=(bsz, DIFF_HEADS, nb),
        in_specs=[
            pl.BlockSpec((1, t, LANES), lambda b, h, i, s: (b, i, h)),
            pl.BlockSpec((1, 1, nb, LANES, t), lambda b, h, i, s: (b, h, 0, 0, 0)),
            pl.BlockSpec((1, seq, LANES), lambda b, h, i, s: (b, 0, h)),
            pl.BlockSpec((8, LANES), lambda b, h, i, s: (0, 0)),
            pl.BlockSpec((1, LANES), lambda b, h, i, s: (0, 0)),
        ],
        out_specs=pl.BlockSpec((1, t, LANES), lambda b, h, i, s: (b, i, h)),
        scratch_shapes=[pltpu.VMEM((t, 2 * LANES), F32), pltpu.VMEM((t, 2 * LANES), F32)],
    )
    return pl.pallas_call(
        functools.partial(_attn_fixed_kernel, t=t, lam_init=lam_init),
        grid_spec=grid_spec,
        out_shape=jax.ShapeDtypeStruct((bsz, seq, DIFF_HEADS * DIFF_V_DIM), BF16),
        compiler_params=_params(("parallel", "parallel", "arbitrary")),
        name="diff_attention_fixed",
    )(slopes_shift, q, kt, v, lam_rows, sub_g)


def _proj_res_kernel(x_ref, w_ref, h_ref, gate_ref, o_ref):
    y = jnp.dot(x_ref[0], w_ref[...], preferred_element_type=F32)
    o_ref[0] = h_ref[0] + gate_ref[0] * y


def _proj_res(x, w, h, gate, tl):
    bsz, seq, kdim = x.shape
    tok = lambda b, t: (b, t, 0)
    return pl.pallas_call(
        _proj_res_kernel,
        grid=(bsz, seq // tl),
        in_specs=[
            pl.BlockSpec((1, tl, kdim), tok),
            pl.BlockSpec((kdim, D_MODEL), lambda b, t: (0, 0)),
            pl.BlockSpec((1, tl, D_MODEL), tok),
            pl.BlockSpec((1, 1, D_MODEL), lambda b, t: (b, 0, 0)),
        ],
        out_specs=pl.BlockSpec((1, tl, D_MODEL), tok),
        out_shape=jax.ShapeDtypeStruct((bsz, seq, D_MODEL), F32),
        compiler_params=_params(("parallel", "parallel")),
        name="attn_out_proj",
    )(x, w, h, gate)


def _moe_route_kernel(h_ref, mod_ref, ng_ref, wr_ref, br_ref, u_ref, route_ref):
    mod = mod_ref[0]
    u = _mod_norm(h_ref[0], ng_ref[...], mod[0:1], mod[1:2])
    _store_row_slabs(u_ref.at[0], u)
    logits = jnp.dot(u.astype(BF16), wr_ref[...], preferred_element_type=F32) + br_ref[...]
    gl = logits[:, :LANES]
    el = logits[:, LANES:]
    lane = lax.broadcasted_iota(jnp.int32, gl.shape, 1)
    lane_f = lane.astype(F32)
    none = float(LANES)

    gmask = lane < N_GROUPS
    glm = jnp.where(gmask, gl, NEG_BIG)
    gmax = jnp.max(glm, axis=-1, keepdims=True)
    g_idx = jnp.min(jnp.where(glm == gmax, lane_f, none), axis=-1, keepdims=True)
    g_w = 1.0 / jnp.sum(jnp.where(gmask, jnp.exp(glm - gmax), 0.0), axis=-1, keepdims=True)

    lo = g_idx * EXPERTS_PER_GROUP
    emask = (lane_f >= lo) & (lane_f < lo + EXPERTS_PER_GROUP)
    elm = jnp.where(emask, el, NEG_BIG)
    emax = jnp.max(elm, axis=-1, keepdims=True)
    ee = jnp.where(emask, jnp.exp(elm - emax), 0.0)
    ep = ee / jnp.sum(ee, axis=-1, keepdims=True)
    epm = jnp.where(emask, ep, -1.0)
    p1 = jnp.max(epm, axis=-1, keepdims=True)
    i1 = jnp.min(jnp.where(epm == p1, lane_f, none), axis=-1, keepdims=True)
    epm2 = jnp.where(lane_f == i1, -1.0, epm)
    p2 = jnp.max(epm2, axis=-1, keepdims=True)
    i2 = jnp.min(jnp.where(epm2 == p2, lane_f, none), axis=-1, keepdims=True)
    denom = p1 + p2
    route_ref[0] = jnp.where(lane == 0, i1, jnp.where(lane == 1, i2, jnp.where(
        lane == 2, p1 / denom * g_w, jnp.where(lane == 3, p2 / denom * g_w, 0.0))))


def _moe_route(h, mod2, ng, wr, br, tl):
    bsz, seq, _ = h.shape
    tok = lambda b, t: (b, t, 0)
    const2 = lambda b, t: (0, 0)
    return pl.pallas_call(
        _moe_route_kernel,
        grid=(bsz, seq // tl),
        in_specs=[
            pl.BlockSpec((1, tl, D_MODEL), tok),
            pl.BlockSpec((1, 2, D_MODEL), lambda b, t: (b, 0, 0)),
            pl.BlockSpec((1, D_MODEL), const2),
            pl.BlockSpec((D_MODEL, 2 * LANES), const2),
            pl.BlockSpec((1, 2 * LANES), const2),
        ],
        out_specs=[pl.BlockSpec((1, tl, ROW_SLABS, LANES), lambda b, t: (b, t, 0, 0)),
                   pl.BlockSpec((1, tl, LANES), tok)],
        out_shape=[jax.ShapeDtypeStruct((bsz, seq, ROW_SLABS, LANES), F32),
                   jax.ShapeDtypeStruct((bsz, seq, LANES), F32)],
        compiler_params=_params(("parallel", "parallel")),
        name="moe_route",
    )(h, mod2, ng, wr, br)


def _moe_schedule(route, tm):
    t = route.shape[0]
    a = 2 * t
    n_tiles = (a // tm + N_EXPERTS + 1) // 2 * 2
    ea = route[:, :2].astype(jnp.int32).reshape(a)
    _, order = lax.sort((ea, jnp.arange(a, dtype=jnp.int32)), num_keys=1, is_stable=True)
    counts = jnp.sum((ea[:, None] == jnp.arange(N_EXPERTS, dtype=jnp.int32)[None, :]).astype(jnp.int32), axis=0)
    padded = (counts + tm - 1) // tm * tm
    pad_end = jnp.cumsum(padded)
    pad_start = pad_end - padded
    start = jnp.cumsum(counts) - counts
    tile_first = jnp.arange(n_tiles, dtype=jnp.int32) * tm
    tile_e = jnp.minimum(jnp.sum((tile_first[:, None] >= pad_end[None, :]).astype(jnp.int32), axis=1),
                         N_EXPERTS - 1)
    slot = jnp.arange(n_tiles * tm, dtype=jnp.int32)
    e_s = jnp.repeat(tile_e, tm)
    local = slot - pad_start[e_s]
    valid = local < counts[e_s]
    asg = order[jnp.clip(start[e_s] + local, 0, a - 1)]
    tok = jnp.where(valid, asg // 2, 0)
    dump = 2 * t + ((slot // tm) % 2) * tm + slot % tm
    dest = jnp.where(valid, (asg % 2) * t + asg // 2, dump)
    dump_tile = 2 * t + tm + jnp.arange(tm, dtype=jnp.int32)
    dest_ext = jnp.concatenate([dump_tile, dest]).reshape(n_tiles + 1, 1, tm)
    return tile_e, tok.reshape(n_tiles, 1, tm), dest_ext


def _moe_ffn_kernel(te_ref, tok0_ref, toka_ref, tokb_ref, desta_ref, destb_ref, destl_ref,
                    wa1_ref, wa3_ref, wa2_ref, wb1_ref, wb3_ref, wb2_ref, u_hbm, y_hbm,
                    x0, x1, y0, y1, w1b, w3b, w2b, gsem, ssem, *, tm, n_tiles):
    j = pl.program_id(0)
    xbuf = (x0, x1)
    ybuf = (y0, y1)

    def start_gather(idx_ref, s):
        for r in range(tm):
            pltpu.make_async_copy(u_hbm.at[pl.ds(idx_ref[0, 0, r], 1)], xbuf[s].at[pl.ds(r, 1)],
                                  gsem.at[s]).start()

    def start_scatter(idx_ref, s):
        for r in range(tm):
            pltpu.make_async_copy(ybuf[s].at[pl.ds(r, 1)], y_hbm.at[pl.ds(idx_ref[0, 0, r], 1)],
                                  ssem.at[s]).start()

    def wait_gather(s):
        pltpu.make_async_copy(u_hbm.at[pl.ds(0, tm)], xbuf[s], gsem.at[s]).wait()

    def wait_scatter(s):
        pltpu.make_async_copy(ybuf[s], y_hbm.at[pl.ds(0, tm)], ssem.at[s]).wait()

    @pl.when(j == 0)
    def _():
        n_real = y_hbm.shape[0] - 2 * tm
        for s in range(2):
            ybuf[s][...] = jnp.zeros(ybuf[s].shape, F32)
            fill = pltpu.make_async_copy(ybuf[s], y_hbm.at[pl.ds(n_real + s * tm, tm)], ssem.at[s])
            fill.start()
            fill.wait()
        start_gather(tok0_ref, 0)

    def tile(t, s, tok_next_ref, dest_prev_ref, first):
        o = 1 - s

        @pl.when((t == 0) | (te_ref[t] != te_ref[jnp.maximum(t - 1, 0)]))
        def _():
            wrefs = (wa1_ref, wa3_ref, wa2_ref) if s == 0 else (wb1_ref, wb3_ref, wb2_ref)
            w1b[...] = wrefs[0][0, 0].astype(BF16)
            w3b[...] = wrefs[1][0, 0].astype(BF16)
            w2b[...] = wrefs[2][0, 0].astype(BF16)

        wait_gather(s)
        if first:
            @pl.when(t > 0)
            def _():
                wait_scatter(s)
        else:
            wait_scatter(s)
        start_scatter(dest_prev_ref, o)
        start_gather(tok_next_ref, o)
        x = _load_row_slabs(xbuf[s]).astype(BF16)
        hid = _silu(jnp.dot(x, w1b[...], preferred_element_type=F32)) * jnp.dot(
            x, w3b[...], preferred_element_type=F32)
        _store_row_slabs(ybuf[s], jnp.dot(hid.astype(BF16), w2b[...], preferred_element_type=F32))

    tile(2 * j, 0, toka_ref, desta_ref, True)
    tile(2 * j + 1, 1, tokb_ref, destb_ref, False)

    @pl.when(j == n_tiles // 2 - 1)
    def _():
        start_scatter(destl_ref, 1)
        wait_scatter(0)
        wait_scatter(1)
        wait_gather(0)


def _moe_ffn(u2d, tile_e, tok, dest_ext, w1, w3, w2, layer, tm):
    t = u2d.shape[0]
    n_tiles = tok.shape[0]
    smem_tile = lambda f: pl.BlockSpec((1, 1, tm), f, memory_space=pltpu.SMEM)
    wspec = lambda shp, par: pl.BlockSpec((1,) + shp, lambda j, te: (layer, te[2 * j + par], 0, 0))
    row_buf = pltpu.VMEM((tm, ROW_SLABS, LANES), F32)
    grid_spec = pltpu.PrefetchScalarGridSpec(
        num_scalar_prefetch=1,
        grid=(n_tiles // 2,),
        in_specs=[
            smem_tile(lambda j, te: (0, 0, 0)),
            smem_tile(lambda j, te: (2 * j + 1, 0, 0)),
            smem_tile(lambda j, te: (jnp.minimum(2 * j + 2, n_tiles - 1), 0, 0)),
            smem_tile(lambda j, te: (2 * j, 0, 0)),
            smem_tile(lambda j, te: (2 * j + 1, 0, 0)),
            smem_tile(lambda j, te: (2 * j + 2, 0, 0)),
            wspec((1, D_MODEL, D_EXPERT), 0),
            wspec((1, D_MODEL, D_EXPERT), 0),
            wspec((1, D_EXPERT, D_MODEL), 0),
            wspec((1, D_MODEL, D_EXPERT), 1),
            wspec((1, D_MODEL, D_EXPERT), 1),
            wspec((1, D_EXPERT, D_MODEL), 1),
            pl.BlockSpec(memory_space=pl.ANY),
        ],
        out_specs=pl.BlockSpec(memory_space=pl.ANY),
        scratch_shapes=[
            row_buf, row_buf, row_buf, row_buf,
            pltpu.VMEM((D_MODEL, D_EXPERT), BF16),
            pltpu.VMEM((D_MODEL, D_EXPERT), BF16),
            pltpu.VMEM((D_EXPERT, D_MODEL), BF16),
            pltpu.SemaphoreType.DMA((2,)),
            pltpu.SemaphoreType.DMA((2,)),
        ],
    )
    return pl.pallas_call(
        functools.partial(_moe_ffn_kernel, tm=tm, n_tiles=n_tiles),
        grid_spec=grid_spec,
        out_shape=jax.ShapeDtypeStruct((2 * t + 2 * tm, ROW_SLABS, LANES), F32),
        compiler_params=_params(("arbitrary",)),
        name="moe_ffn",
    )(tile_e, tok, tok, tok, dest_ext, dest_ext, dest_ext, w1, w3, w2, w1, w3, w2, u2d)


def _moe_combine_kernel(h_ref, gate_ref, route_ref, ya_ref, yb_ref, o_ref):
    route = route_ref[0]
    lane = lax.broadcasted_iota(jnp.int32, route.shape, 1)
    wa = jnp.sum(jnp.where(lane == 2, route, 0.0), axis=-1, keepdims=True)
    wb = jnp.sum(jnp.where(lane == 3, route, 0.0), axis=-1, keepdims=True)
    o_ref[0] = h_ref[0] + gate_ref[0] * (wa * _load_row_slabs(ya_ref) + wb * _load_row_slabs(yb_ref))


def _moe_combine(h, gate, route, y2, tl):
    bsz, seq, _ = h.shape
    nt = seq // tl
    tok = lambda b, t: (b, t, 0)
    return pl.pallas_call(
        _moe_combine_kernel,
        grid=(bsz, nt),
        in_specs=[
            pl.BlockSpec((1, tl, D_MODEL), tok),
            pl.BlockSpec((1, 1, D_MODEL), lambda b, t: (b, 0, 0)),
            pl.BlockSpec((1, tl, LANES), tok),
            pl.BlockSpec((tl, ROW_SLABS, LANES), lambda b, t: (b * nt + t, 0, 0)),
            pl.BlockSpec((tl, ROW_SLABS, LANES), lambda b, t: (bsz * nt + b * nt + t, 0, 0)),
        ],
        out_specs=pl.BlockSpec((1, tl, D_MODEL), tok),
        out_shape=jax.ShapeDtypeStruct((bsz, seq, D_MODEL), F32),
        compiler_params=_params(("parallel", "parallel")),
        name="moe_combine",
    )(h, gate, route, y2, y2)


def _alibi_slopes():
    start = 2.0 ** (-8.0 / DIFF_HEADS)
    return jnp.asarray([start ** (i + 1) for i in range(DIFF_HEADS)], F32)


def _head_expand_matrix():
    head_of_lane = jnp.arange(D_INNER) // SSM_HEAD_DIM
    e = (jnp.arange(DT_PAD)[:, None] == head_of_lane[None, :]).astype(BF16)
    return jnp.concatenate([e, e], axis=0)


def _pad_lanes(x, width):
    return jnp.pad(x, [(0, 0)] * (x.ndim - 1) + [(0, width - x.shape[-1])])


def _prep_weights(p):
    w = {}
    w_in = p["ssm_w_in"]
    w["ssm_w_in"] = jnp.concatenate(
        [w_in[..., :D_INNER + CONV_DIM], _pad_lanes(w_in[..., D_INNER + CONV_DIM:], DT_PAD)], axis=-1).astype(BF16)
    w["ssm_dt_bias"] = _pad_lanes(p["ssm_dt_bias"], DT_PAD)[:, None, :]
    w["ssm_a_log"] = _pad_lanes(p["ssm_a_log"], DT_PAD)[:, None, :]
    w["ssm_d"] = jnp.repeat(p["ssm_d"], SSM_HEAD_DIM, axis=-1)[:, None, :]
    w["ssm_w_out"] = p["ssm_w_out"].astype(BF16)
    w["e2"] = _head_expand_matrix()
    w["w_kv"] = p["w_kv"].astype(BF16)
    w["k_norm_g"] = jnp.tile(p["k_norm_g"], 2 * DIFF_HEADS)[None, :]
    w["attn_w_q"] = p["attn_w_q"].astype(BF16)
    w["attn_q_norm_g"] = jnp.tile(p["attn_q_norm_g"], (1, 2 * DIFF_HEADS))[:, None, :]
    lam = jnp.stack([p["attn_lambda_q1"], p["attn_lambda_k1"], p["attn_lambda_q2"], p["attn_lambda_k2"]], axis=1)
    w["lam_rows"] = jnp.pad(lam, ((0, 0), (0, 4), (0, LANES - DIFF_HEAD_DIM)))
    w["attn_w_o"] = p["attn_w_o"].astype(BF16)
    wr = jnp.concatenate([_pad_lanes(p["moe_w_group"], LANES), _pad_lanes(p["moe_w_expert"], LANES)], axis=-1)
    br = jnp.concatenate([_pad_lanes(p["moe_b_group"], LANES), _pad_lanes(p["moe_b_expert"], LANES)], axis=-1)
    w["moe_wr"] = wr.astype(BF16)
    w["moe_br"] = br[:, None, :]
    return w


def _trunk(x, mods, kv_mod, conv_prev, ssm_prev, k_past, v_past, p, w, cfg):
    bsz, seq, _ = x.shape
    tl, lt, lc, tq, tk = cfg["tl"], cfg["lt"], cfg["lc"], cfg["tq"], cfg["tk"]
    pos0 = 0 if k_past is None else k_past.shape[1]
    h = x
    conv_new, ssm_new = [], []
    k_new = v_new = k_all = v_all = None
    for layer in range(DEPTH):
        mod = mods[layer]
        if layer < N_A_LAYERS:
            i = layer
            cprev = jnp.pad(conv_prev[i], ((0, 0), (8 - (CONV_WIDTH - 1), 0), (0, 0)))
            sprev = ssm_prev[i].reshape(bsz, D_INNER, SSM_STATE)
            h, cn, sn = _mamba_layer(
                h, mod, p["norm_g"][layer, 0][None, :], w["ssm_w_in"][i], p["ssm_conv_w"][i],
                p["ssm_conv_b"][i][None, :], w["ssm_dt_bias"][i], w["ssm_a_log"][i], w["ssm_d"][i],
                p["ssm_norm_g"][i][None, :], w["ssm_w_out"][i], cprev, sprev, w["e2"], lt, lc)
            conv_new.append(cn[:, 8 - (CONV_WIDTH - 1):, :])
            ssm_new.append(sn.reshape(bsz, SSM_HEADS, SSM_HEAD_DIM, SSM_STATE))
        else:
            j = layer - N_A_LAYERS
            lam_init = 0.8 - 0.6 * math.exp(-0.3 * layer)
            q = _normproj(h, mod[:, 0:2], p["norm_g"][layer, 0][None, :], w["attn_w_q"][j],
                          w["attn_q_norm_g"][j], tl, False, "attn_q_proj")[0]
            slopes = _alibi_slopes()
            sub_g = p["attn_sub_g"][j][None, :]
            online = functools.partial(_diff_attention, slopes=slopes, lam_rows=w["lam_rows"][j], sub_g=sub_g,
                                       tq=tq, tk=tk, pos0=pos0, lam_init=lam_init)
            if k_past is None and tq == tk:
                bound = (DIFF_HEAD_DIM ** 0.5 * (1.0 + 2.0 ** -6) * jnp.max(jnp.abs(p["attn_q_norm_g"][j]))
                         * jnp.max(jnp.abs(p["k_norm_g"])))
                slopes_shift = jnp.concatenate([slopes, bound[None], jnp.zeros((7,), F32)])
                o = lax.cond(
                    bound <= FIXED_SHIFT_LIMIT,
                    lambda q_, k_, v_: _diff_attention_fixed(q_, k_, v_, slopes_shift, w["lam_rows"][j], sub_g,
                                                             tq, lam_init),
                    lambda q_, k_, v_: online(q_, k_, v_),
                    q, k_all, v_all)
            else:
                o = online(q, k_all, v_all)
            h = _proj_res(o, w["attn_w_o"][j], h, mod[:, 2:3], tl)
        u, route = _moe_route(h, mod[:, 3:5], p["norm_g"][layer, 1][None, :], w["moe_wr"][layer],
                              w["moe_br"][layer], tl)
        tile_e, tok, dest_ext = _moe_schedule(route.reshape(bsz * seq, LANES), cfg["tm"])
        y2 = _moe_ffn(u.reshape(bsz * seq, ROW_SLABS, LANES), tile_e, tok, dest_ext, p["moe_w1"], p["moe_w3"],
                      p["moe_w2"], layer, cfg["tm"])
        h = _moe_combine(h, mod[:, 5:6], route, y2, tl)
        if layer == N_A_LAYERS - 1:
            kv_f32, kv_bf = _normproj(h, kv_mod, p["kv_norm_g"][None, :], w["w_kv"], w["k_norm_g"], tl, True,
                                      "shared_kv")
            nk_cols = DIFF_HEADS * 2 * DIFF_HEAD_DIM
            k_new = kv_f32[..., :nk_cols].reshape(bsz, seq, DIFF_HEADS, 2, DIFF_HEAD_DIM)
            v_new = kv_f32[..., nk_cols:].reshape(bsz, seq, DIFF_HEADS, DIFF_V_DIM)
            k_all, v_all = kv_bf[..., :nk_cols], kv_bf[..., nk_cols:]
            if k_past is not None:
                k_all = jnp.concatenate([k_past.reshape(bsz, pos0, nk_cols).astype(BF16), k_all], axis=1)
                v_all = jnp.concatenate([v_past.reshape(bsz, pos0, nk_cols).astype(BF16), v_all], axis=1)
                pad = (-k_all.shape[1]) % tk
                k_all = jnp.pad(k_all, ((0, 0), (0, pad), (0, 0)))
                v_all = jnp.pad(v_all, ((0, 0), (0, pad), (0, 0)))
    return h, jnp.stack(conv_new), jnp.stack(ssm_new), k_new, v_new


PROMPT_CFG = dict(tl=512, lt=256, lc=64, tq=512, tk=512, tm=256)
SAMPLE_CFG = dict(tl=64, lt=64, lc=64, tq=64, tk=1152, tm=64)


def kernel(x_prompt, x_sample, c_prompt, c_sample, state_conv, state_ssm, cache_k, cache_v, ada_w, ada_b, norm_g, ssm_w_in, ssm_conv_w, ssm_conv_b, ssm_dt_bias, ssm_a_log, ssm_d, ssm_norm_g, ssm_w_out, kv_norm_g, kv_ada_w, kv_ada_b, w_kv, k_norm_g, attn_w_q, attn_q_norm_g, attn_lambda_q1, attn_lambda_k1, attn_lambda_q2, attn_lambda_k2, attn_sub_g, attn_w_o, moe_w_group, moe_b_group, moe_w_expert, moe_b_expert, moe_w1, moe_w3, moe_w2):
    p = dict(norm_g=norm_g, ssm_w_in=ssm_w_in, ssm_conv_w=ssm_conv_w, ssm_conv_b=ssm_conv_b,
             ssm_dt_bias=ssm_dt_bias, ssm_a_log=ssm_a_log, ssm_d=ssm_d, ssm_norm_g=ssm_norm_g,
             ssm_w_out=ssm_w_out, kv_norm_g=kv_norm_g, w_kv=w_kv, k_norm_g=k_norm_g, attn_w_q=attn_w_q,
             attn_q_norm_g=attn_q_norm_g, attn_lambda_q1=attn_lambda_q1, attn_lambda_k1=attn_lambda_k1,
             attn_lambda_q2=attn_lambda_q2, attn_lambda_k2=attn_lambda_k2, attn_sub_g=attn_sub_g,
             attn_w_o=attn_w_o, moe_w_group=moe_w_group, moe_b_group=moe_b_group, moe_w_expert=moe_w_expert,
             moe_b_expert=moe_b_expert, moe_w1=moe_w1, moe_w3=moe_w3, moe_w2=moe_w2)
    w = _prep_weights(p)

    bp, bs = x_prompt.shape[0], x_sample.shape[0]
    c_all = jnp.concatenate([c_prompt, c_sample], axis=0)
    c_all = jnp.pad(c_all, ((0, (-c_all.shape[0]) % 8), (0, 0)))
    mods = _mods(c_all, ada_w, ada_b[:, None, :], 1536).reshape(DEPTH, -1, 6, D_MODEL)
    kv_mods = _mods(c_all, kv_ada_w[None], kv_ada_b[None, None, :], 1024).reshape(-1, 2, D_MODEL)

    conv0 = jnp.zeros((N_A_LAYERS, bp, CONV_WIDTH - 1, CONV_DIM), F32)
    ssm0 = jnp.zeros((N_A_LAYERS, bp, SSM_HEADS, SSM_HEAD_DIM, SSM_STATE), F32)
    y_p, conv_p, ssm_p, k_p, v_p = _trunk(x_prompt, mods[:, :bp], kv_mods[:bp], conv0, ssm0, None, None,
                                          p, w, PROMPT_CFG)
    y_s, conv_s, ssm_s, k_s, v_s = _trunk(x_sample, mods[:, bp:bp + bs], kv_mods[bp:bp + bs], state_conv,
                                          state_ssm, cache_k, cache_v, p, w, SAMPLE_CFG)
    return (y_p, y_s, conv_p, ssm_p, k_p, v_p, conv_s, ssm_s, k_s, v_s)
```

```python
import functools
import math

import jax
import jax.numpy as jnp
from jax import lax
from jax.experimental import pallas as pl
from jax.experimental.pallas import tpu as pltpu

F32 = jnp.float32
BF16 = jnp.bfloat16
HIGHEST = lax.Precision.HIGHEST

D_MODEL = 1024
DEPTH = 4
N_A_LAYERS = 2
EPS = 1e-6
CHUNK = 64
D_INNER = 2048
SSM_HEADS = 32
SSM_HEAD_DIM = 64
SSM_GROUPS = 4
SSM_STATE = 128
GROUP_WIDTH = D_INNER // SSM_GROUPS
CONV_WIDTH = 4
CONV_DIM = D_INNER + 2 * SSM_GROUPS * SSM_STATE
DT_PAD = 128
IN_PROJ_PAD = D_INNER + CONV_DIM + DT_PAD
DIFF_HEADS = 8
DIFF_HEAD_DIM = 64
DIFF_V_DIM = 128
N_GROUPS = 4
EXPERTS_PER_GROUP = 8
N_EXPERTS = 32
D_EXPERT = 512

LANES = 128
ROW_SLABS = D_MODEL // LANES
NEG_BIG = -1e30
FIXED_SHIFT_LIMIT = 40.0
VMEM_LIMIT = 56 * 1024 * 1024


def _params(semantics, vmem=VMEM_LIMIT):
    return pltpu.CompilerParams(dimension_semantics=semantics, vmem_limit_bytes=vmem)


def _silu(x):
    return x * (1.0 / (1.0 + jnp.exp(-x)))


def _softplus(x):
    return jnp.maximum(x, 0.0) + jnp.log1p(jnp.exp(-jnp.abs(x)))


def _mod_norm(h, gain, shift, scale):
    ms = jnp.mean(h * h, axis=-1, keepdims=True)
    return (h * lax.rsqrt(ms + EPS) * gain) * (1.0 + scale) + shift


def _store_row_slabs(ref, x):
    for s in range(ROW_SLABS):
        ref[:, s, :] = x[:, s * LANES:(s + 1) * LANES]


def _load_row_slabs(ref):
    return jnp.concatenate([ref[:, s, :] for s in range(ROW_SLABS)], axis=1)


def _seg_rms64(x, g, n_cols):
    lane = lax.broadcasted_iota(jnp.int32, (x.shape[0], LANES), 1)
    low = lane < 64
    outs = []
    for cb in range(n_cols // LANES):
        xc = x[:, cb * LANES:(cb + 1) * LANES]
        sq = xc * xc
        s_lo = jnp.sum(jnp.where(low, sq, 0.0), axis=-1, keepdims=True)
        s_hi = jnp.sum(jnp.where(low, 0.0, sq), axis=-1, keepdims=True)
        r = jnp.where(low, lax.rsqrt(s_lo * (1.0 / 64) + EPS), lax.rsqrt(s_hi * (1.0 / 64) + EPS))
        outs.append(xc * r * g[:, cb * LANES:(cb + 1) * LANES])
    if n_cols < x.shape[1]:
        outs.append(x[:, n_cols:])
    return jnp.concatenate(outs, axis=1)


def _mods_kernel(c_ref, w_ref, b_ref, o_ref):
    s = _silu(c_ref[...])
    o_ref[0] = jnp.dot(s.astype(BF16), w_ref[0].astype(BF16), preferred_element_type=F32) + b_ref[0]


def _mods(c_all, w, b, tn):
    nl, _, n = w.shape
    m = c_all.shape[0]
    return pl.pallas_call(
        _mods_kernel,
        grid=(nl, n // tn),
        in_specs=[
            pl.BlockSpec((m, D_MODEL), lambda l, j: (0, 0)),
            pl.BlockSpec((1, D_MODEL, tn), lambda l, j: (l, 0, j)),
            pl.BlockSpec((1, 1, tn), lambda l, j: (l, 0, j)),
        ],
        out_specs=pl.BlockSpec((1, m, tn), lambda l, j: (l, 0, j)),
        out_shape=jax.ShapeDtypeStruct((nl, m, n), F32),
        compiler_params=_params(("parallel", "parallel")),
        name="adaln_mods",
    )(c_all, w, b)


def _expand_heads(v, e2_ref):
    hi = v.astype(BF16)
    lo = (v - hi.astype(F32)).astype(BF16)
    return jnp.dot(jnp.concatenate([hi, lo], axis=1), e2_ref[...], preferred_element_type=F32)


def _mamba_kernel(h_ref, mod_ref, ng_ref, w_ref, cw_ref, cb_ref, dtb_ref, alog_ref, dsk_ref, sng_ref,
                  wo_ref, cprev_ref, sprev_ref, e2_ref,
                  hout_ref, cnew_ref, snew_ref,
                  xbc_scr, st_scr, xd_scr, y_scr, *, lt, lc, nc):
    c = pl.program_id(1)

    @pl.when(c == 0)
    def _():
        xbc_scr[...] = cprev_ref[0]
        st_scr[...] = sprev_ref[0].T

    h = h_ref[0]
    mod = mod_ref[0]
    u = _mod_norm(h, ng_ref[...], mod[0:1], mod[1:2])
    proj = jnp.dot(u.astype(BF16), w_ref[...], preferred_element_type=F32)
    z = proj[:, :D_INNER]
    xbc_raw = proj[:, D_INNER:D_INNER + CONV_DIM]
    dt_raw = proj[:, D_INNER + CONV_DIM:]

    cw = cw_ref[...]
    prev8 = xbc_scr[...]
    row8 = lax.broadcasted_iota(jnp.int32, (8, CONV_DIM), 0)
    conv = cb_ref[...] + cw[CONV_WIDTH - 1:CONV_WIDTH] * xbc_raw
    for s in range(1, CONV_WIDTH):
        rolled = pltpu.roll(xbc_raw, s, 0)
        first8 = jnp.where(row8 < s, pltpu.roll(prev8, s, 0), rolled[0:8])
        tap = CONV_WIDTH - 1 - s
        conv = conv + cw[tap:tap + 1] * jnp.concatenate([first8, rolled[8:]], axis=0)
    tail = xbc_raw[lt - 8:lt]
    cnew_ref[0] = tail
    xbc_scr[...] = tail
    xbc = _silu(conv)
    xs = xbc[:, :D_INNER]
    bm = xbc[:, D_INNER:D_INNER + GROUP_WIDTH]
    cm = xbc[:, D_INNER + GROUP_WIDTH:].astype(BF16)

    dt = _softplus(dt_raw + dtb_ref[...])
    da = dt * (-jnp.exp(alog_ref[...]))
    xd_scr[...] = xs * _expand_heads(dt, e2_ref)

    row_i = lax.broadcasted_iota(jnp.int32, (lc, lc), 0)
    col_j = lax.broadcasted_iota(jnp.int32, (lc, lc), 1)
    tril = col_j <= row_i
    trilf = tril.astype(F32)
    low = lax.broadcasted_iota(jnp.int32, (lc, LANES), 1) < SSM_HEAD_DIM

    for s in range(lt // lc):
        r0 = s * lc
        acum = jnp.dot(trilf, da[r0:r0 + lc], precision=HIGHEST, preferred_element_type=F32)
        acum_t = acum.T
        acum_e = _expand_heads(acum, e2_ref)
        ea = jnp.exp(acum_e)
        dend = jnp.exp(acum_e[lc - 1:lc] - acum_e)
        xd_s = xd_scr[r0:r0 + lc, :]
        xdb = xd_s.astype(BF16)
        xdw = (xd_s * dend).astype(BF16)
        for g in range(SSM_GROUPS):
            gl = slice(g * GROUP_WIDTH, (g + 1) * GROUP_WIDTH)
            b_g = bm[r0:r0 + lc, g * SSM_STATE:(g + 1) * SSM_STATE]
            c_g = cm[r0:r0 + lc, g * SSM_STATE:(g + 1) * SSM_STATE]
            cbm = lax.dot_general(c_g, b_g.astype(BF16), (((1,), (1,)), ((), ())),
                                  preferred_element_type=F32)
            s_g = st_scr[:, gl]
            yoff = jnp.dot(c_g, s_g.astype(BF16), preferred_element_type=F32) * ea[:, gl]
            for j in range(GROUP_WIDTH // LANES):
                pair = g * (GROUP_WIDTH // LANES) + j
                pl_ = slice(pair * LANES, (pair + 1) * LANES)
                ys = []
                for hh in (2 * pair, 2 * pair + 1):
                    seg = acum[:, hh:hh + 1] - acum_t[hh:hh + 1, :]
                    dec = jnp.exp(jnp.where(tril, seg, NEG_BIG))
                    ys.append(jnp.dot((cbm * dec).astype(BF16), xdb[:, pl_], preferred_element_type=F32))
                y_scr[r0:r0 + lc, pl_] = jnp.where(low, ys[0], ys[1]) + yoff[:, j * LANES:(j + 1) * LANES]
            upd = jnp.dot(b_g.T.astype(BF16), xdw[:, gl], preferred_element_type=F32)
            st_scr[:, gl] = s_g * ea[lc - 1:lc, gl] + upd

    y = (y_scr[...] + dsk_ref[...] * xs) * _silu(z)
    parts = []
    for g in range(SSM_GROUPS):
        gl = slice(g * GROUP_WIDTH, (g + 1) * GROUP_WIDTH)
        yg = y[:, gl]
        ms = jnp.mean(yg * yg, axis=-1, keepdims=True)
        parts.append(yg * lax.rsqrt(ms + EPS) * sng_ref[:, gl])
    yn = jnp.concatenate(parts, axis=1).astype(BF16)
    out = jnp.dot(yn, wo_ref[...], preferred_element_type=F32)
    hout_ref[0] = h + mod[2:3] * out

    @pl.when(c == nc - 1)
    def _():
        snew_ref[0] = st_scr[...].T


def _mamba_layer(h, mod, ng, w_in, cw, cb, dtb, alog, dsk, sng, wo, cprev, sprev, e2, lt, lc):
    bsz, seq, _ = h.shape
    nc = seq // lt
    const2 = lambda b, c: (0, 0)
    perb = lambda b, c: (b, 0, 0)
    kern = functools.partial(_mamba_kernel, lt=lt, lc=lc, nc=nc)
    return pl.pallas_call(
        kern,
        grid=(bsz, nc),
        in_specs=[
            pl.BlockSpec((1, lt, D_MODEL), lambda b, c: (b, c, 0)),
            pl.BlockSpec((1, 6, D_MODEL), perb),
            pl.BlockSpec((1, D_MODEL), const2),
            pl.BlockSpec((D_MODEL, IN_PROJ_PAD), const2),
            pl.BlockSpec((CONV_WIDTH, CONV_DIM), const2),
            pl.BlockSpec((1, CONV_DIM), const2),
            pl.BlockSpec((1, DT_PAD), const2),
            pl.BlockSpec((1, DT_PAD), const2),
            pl.BlockSpec((1, D_INNER), const2),
            pl.BlockSpec((1, D_INNER), const2),
            pl.BlockSpec((D_INNER, D_MODEL), const2),
            pl.BlockSpec((1, 8, CONV_DIM), perb),
            pl.BlockSpec((1, D_INNER, SSM_STATE), perb),
            pl.BlockSpec((2 * DT_PAD, D_INNER), const2),
        ],
        out_specs=[
            pl.BlockSpec((1, lt, D_MODEL), lambda b, c: (b, c, 0)),
            pl.BlockSpec((1, 8, CONV_DIM), perb),
            pl.BlockSpec((1, D_INNER, SSM_STATE), perb),
        ],
        out_shape=[
            jax.ShapeDtypeStruct((bsz, seq, D_MODEL), F32),
            jax.ShapeDtypeStruct((bsz, 8, CONV_DIM), F32),
            jax.ShapeDtypeStruct((bsz, D_INNER, SSM_STATE), F32),
        ],
        scratch_shapes=[
            pltpu.VMEM((8, CONV_DIM), F32),
            pltpu.VMEM((SSM_STATE, D_INNER), F32),
            pltpu.VMEM((lt, D_INNER), F32),
            pltpu.VMEM((lt, D_INNER), F32),
        ],
        compiler_params=_params(("parallel", "arbitrary")),
        name="mamba_layer",
    )(h, mod, ng, w_in, cw, cb, dtb, alog, dsk, sng, wo, cprev, sprev, e2)


def _normproj_kernel(h_ref, mod_ref, ng_ref, w_ref, sg_ref, *out_refs, n_norm, want_f32):
    mod = mod_ref[0]
    u = _mod_norm(h_ref[0], ng_ref[...], mod[0:1], mod[1:2])
    y = jnp.dot(u.astype(BF16), w_ref[...], preferred_element_type=F32)
    y = _seg_rms64(y, sg_ref[...], n_norm)
    n_parts = y.shape[1] // D_MODEL
    for part in range(n_parts):
        yp = y[:, part * D_MODEL:(part + 1) * D_MODEL]
        if want_f32:
            out_refs[part][0] = yp
        out_refs[part - n_parts][0] = yp.astype(BF16)


def _normproj(h, mod2, ng, w, seg_g, tl, want_f32, name):
    bsz, seq, _ = h.shape
    nout = w.shape[1]
    n_norm = seg_g.shape[1]
    n_parts = nout // D_MODEL
    tok = lambda b, t: (b, t, 0)
    const2 = lambda b, t: (0, 0)
    out_specs = [pl.BlockSpec((1, tl, D_MODEL), tok)] * n_parts
    out_shape = [jax.ShapeDtypeStruct((bsz, seq, D_MODEL), BF16)] * n_parts
    if want_f32:
        out_specs = [pl.BlockSpec((1, tl, D_MODEL), tok)] * n_parts + out_specs
        out_shape = [jax.ShapeDtypeStruct((bsz, seq, D_MODEL), F32)] * n_parts + out_shape
    return pl.pallas_call(
        functools.partial(_normproj_kernel, n_norm=n_norm, want_f32=want_f32),
        grid=(bsz, seq // tl),
        in_specs=[
            pl.BlockSpec((1, tl, D_MODEL), tok),
            pl.BlockSpec((1, 2, D_MODEL), lambda b, t: (b, 0, 0)),
            pl.BlockSpec((1, D_MODEL), const2),
            pl.BlockSpec((D_MODEL, nout), const2),
            pl.BlockSpec((1, n_norm), const2),
        ],
        out_specs=out_specs,
        out_shape=out_shape,
        compiler_params=_params(("parallel", "parallel")),
        name=name,
    )(h, mod2, ng, w, seg_g)


def _attn_kernel(slope_ref, q_ref, k_ref, v_ref, lam_ref, sg_ref, o_ref,
                 m1, l1, a1, m2, l2, a2, *, tq, tk, nk, pos0, lam_init):
    hd = pl.program_id(1)
    qi = pl.program_id(2)
    ki = pl.program_id(3)

    @pl.when(ki == 0)
    def _():
        for m, l, a in ((m1, l1, a1), (m2, l2, a2)):
            m[...] = jnp.full(m.shape, NEG_BIG, F32)
            l[...] = jnp.zeros(l.shape, F32)
            a[...] = jnp.zeros(a.shape, F32)

    q_first = pos0 + qi * tq
    last_q_chunk = (q_first + tq - 1) // CHUNK
    first_k_chunk = (ki * tk) // CHUNK

    @pl.when(first_k_chunk <= last_q_chunk)
    def _():
        q = q_ref[0]
        k = k_ref[0]
        v = v_ref[0]
        lane = lax.broadcasted_iota(jnp.int32, q.shape, 1)
        zero = jnp.zeros_like(q)
        qa = jnp.where(lane < DIFF_HEAD_DIM, q, zero)
        qb = jnp.where(lane < DIFF_HEAD_DIM, zero, q)
        qpos = q_first + lax.broadcasted_iota(jnp.int32, (tq, tk), 0)
        kpos = ki * tk + lax.broadcasted_iota(jnp.int32, (tq, tk), 1)
        visible = (kpos // CHUNK) <= (qpos // CHUNK)
        dist = jnp.abs(qpos - kpos).astype(F32)
        bias = jnp.where(visible, -slope_ref[hd] * dist, NEG_BIG)
        scale = DIFF_HEAD_DIM ** -0.5
        nt = (((1,), (1,)), ((), ()))
        for qq, m, l, a in ((qa, m1, l1, a1), (qb, m2, l2, a2)):
            s = lax.dot_general(qq, k, nt, preferred_element_type=F32) * scale + bias
            m_old = m[...]
            m_new = jnp.maximum(m_old, jnp.max(s, axis=-1, keepdims=True))
            p = jnp.exp(s - m_new)
            alpha = jnp.exp(m_old - m_new)
            l[...] = alpha * l[...] + jnp.sum(p, axis=-1, keepdims=True)
            a[...] = alpha * a[...] + jnp.dot(p.astype(BF16), v, preferred_element_type=F32)
            m[...] = m_new

    @pl.when(ki == nk - 1)
    def _():
        lp = lam_ref[...]
        lam = (jnp.exp(jnp.sum(lp[0:1] * lp[1:2], axis=-1, keepdims=True))
               - jnp.exp(jnp.sum(lp[2:3] * lp[3:4], axis=-1, keepdims=True)) + lam_init)
        o = a1[...] / l1[...] - lam * (a2[...] / l2[...])
        ms = jnp.mean(o * o, axis=-1, keepdims=True)
        o_ref[0] = (o * lax.rsqrt(ms + EPS) * sg_ref[...] * (1.0 - lam_init)).astype(BF16)


def _diff_attention(q, k, v, slopes, lam_rows, sub_g, tq, tk, pos0, lam_init):
    bsz, lq, _ = q.shape
    lk = k.shape[1]
    nq, nk = lq // tq, lk // tk

    def kv_map(b, h, i, j, slope_ref):
        last_visible = ((pos0 + i * tq + tq - 1) // CHUNK * CHUNK + CHUNK - 1) // tk
        return (b, jnp.minimum(j, last_visible), h)

    grid_spec = pltpu.PrefetchScalarGridSpec(
        num_scalar_prefetch=1,
        grid=(bsz, DIFF_HEADS, nq, nk),
        in_specs=[
            pl.BlockSpec((1, tq, LANES), lambda b, h, i, j, s: (b, i, h)),
            pl.BlockSpec((1, tk, LANES), kv_map),
            pl.BlockSpec((1, tk, LANES), kv_map),
            pl.BlockSpec((8, LANES), lambda b, h, i, j, s: (0, 0)),
            pl.BlockSpec((1, LANES), lambda b, h, i, j, s: (0, 0)),
        ],
        out_specs=pl.BlockSpec((1, tq, LANES), lambda b, h, i, j, s: (b, i, h)),
        scratch_shapes=[
            pltpu.VMEM((tq, 1), F32), pltpu.VMEM((tq, 1), F32), pltpu.VMEM((tq, LANES), F32),
            pltpu.VMEM((tq, 1), F32), pltpu.VMEM((tq, 1), F32), pltpu.VMEM((tq, LANES), F32),
        ],
    )
    return pl.pallas_call(
        functools.partial(_attn_kernel, tq=tq, tk=tk, nk=nk, pos0=pos0, lam_init=lam_init),
        grid_spec=grid_spec,
        out_shape=jax.ShapeDtypeStruct((bsz, lq, DIFF_HEADS * DIFF_V_DIM), BF16),
        compiler_params=_params(("parallel", "parallel", "parallel", "arbitrary")),
        name="diff_attention",
    )(slopes, q, k, v, lam_rows, sub_g)


def _attn_fixed_kernel(sc_ref, q_ref, kt_ref, v_ref, lam_ref, sg_ref, o_ref, acc1, acc2, *, t, lam_init):
    hd = pl.program_id(1)
    qi = pl.program_id(2)
    slope = sc_ref[hd]
    shift = sc_ref[DIFF_HEADS]

    lane = lax.broadcasted_iota(jnp.int32, (t, LANES), 1)
    qoff = lax.broadcasted_iota(jnp.int32, (t, LANES), 0)
    q_hi = (qoff >> 8).astype(F32) * 256.0
    q_lo = (qoff & 255).astype(F32)
    aug_q = jnp.where(lane == 0, -slope * q_hi, jnp.where(lane == 1, -slope * q_lo, jnp.where(
        (lane == 2) | (lane == 3), 1.0, jnp.where(lane == 4, -shift, 0.0)))).astype(BF16)
    qs = q_ref[0] * jnp.asarray(DIFF_HEAD_DIM ** -0.5, BF16)
    zero = jnp.zeros_like(qs)
    qa = jnp.concatenate([jnp.where(lane < DIFF_HEAD_DIM, qs, zero), aug_q], axis=1)
    qb = jnp.concatenate([jnp.where(lane < DIFF_HEAD_DIM, zero, qs), aug_q], axis=1)
    acc1[...] = jnp.zeros(acc1.shape, F32)
    acc2[...] = jnp.zeros(acc2.shape, F32)
    ones = jnp.ones((t, LANES), BF16)
    aug_row = lax.broadcasted_iota(jnp.int32, (16, t), 0)
    koff = lax.broadcasted_iota(jnp.int32, (16, t), 1)
    pad_rows = jnp.zeros((LANES - 16, t), BF16)

    def accumulate(kb, aug_k, bias):
        k0 = pl.multiple_of(kb * t, t)
        kfull = jnp.concatenate([kt_ref[0, 0, kb], aug_k.astype(BF16), pad_rows], axis=0)
        v2 = jnp.concatenate([v_ref[0, pl.ds(k0, t), :], ones], axis=1)
        for qq, acc in ((qa, acc1), (qb, acc2)):
            s = jnp.dot(qq, kfull, preferred_element_type=F32)
            if bias is not None:
                s = s + bias
            acc[...] += jnp.dot(jnp.exp(s).astype(BF16), v2, preferred_element_type=F32)

    def below_diagonal(kb, carry):
        rel = (kb - qi) * t + koff
        k_hi = (rel >> 8).astype(F32) * 256.0
        k_lo = (rel & 255).astype(F32)
        aug_k = jnp.where(aug_row <= 1, 1.0, jnp.where(aug_row == 2, slope * k_hi, jnp.where(
            aug_row == 3, slope * k_lo, jnp.where(aug_row == 4, 1.0, 0.0))))
        accumulate(kb, aug_k, None)
        return carry

    lax.fori_loop(0, qi, below_diagonal, 0)

    qo = lax.broadcasted_iota(jnp.int32, (t, t), 0)
    ko = lax.broadcasted_iota(jnp.int32, (t, t), 1)
    visible = (ko // CHUNK) <= (qo // CHUNK)
    bias = jnp.where(visible, -slope * jnp.abs(qo - ko).astype(F32), NEG_BIG)
    accumulate(qi, jnp.where(aug_row == 4, 1.0, 0.0), bias)

    lp = lam_ref[...]
    lam = (jnp.exp(jnp.sum(lp[0:1] * lp[1:2], axis=-1, keepdims=True))
           - jnp.exp(jnp.sum(lp[2:3] * lp[3:4], axis=-1, keepdims=True)) + lam_init)
    a1 = acc1[...]
    a2 = acc2[...]
    o = a1[:, :LANES] / a1[:, LANES:] - lam * (a2[:, :LANES] / a2[:, LANES:])
    ms = jnp.mean(o * o, axis=-1, keepdims=True)
    o_ref[0] = (o * lax.rsqrt(ms + EPS) * sg_ref[...] * (1.0 - lam_init)).astype(BF16)


def _key_blocks_transposed(k, t):
    bsz, seq, _ = k.shape
    return k.reshape(bsz, seq // t, t, DIFF_HEADS, LANES).transpose(0, 3, 1, 4, 2)


def _diff_attention_fixed(q, kt, v, slopes_shift, lam_rows, sub_g, t, lam_init):
    bsz, seq, _ = q.shape
    nb = seq // t
    grid_spec = pltpu.PrefetchScalarGridSpec(
        num_scalar_prefetch=1,
        grid=(bsz, DIFF_HEADS, nb),
        in_specs=[
            pl.BlockSpec((1, t, LANES), lambda b, h, i, s: (b, i, h)),
            pl.BlockSpec((1, 1, nb, LANES, t), lambda b, h, i, s: (b, h, 0, 0, 0)),
            pl.BlockSpec((1, seq, LANES), lambda b, h, i, s: (b, 0, h)),
            pl.BlockSpec((8, LANES), lambda b, h, i, s: (0, 0)),
            pl.BlockSpec((1, LANES), lambda b, h, i, s: (0, 0)),
        ],
        out_specs=pl.BlockSpec((1, t, LANES), lambda b, h, i, s: (b, i, h)),
        scratch_shapes=[pltpu.VMEM((t, 2 * LANES), F32), pltpu.VMEM((t, 2 * LANES), F32)],
    )
    return pl.pallas_call(
        functools.partial(_attn_fixed_kernel, t=t, lam_init=lam_init),
        grid_spec=grid_spec,
        out_shape=jax.ShapeDtypeStruct((bsz, seq, DIFF_HEADS * DIFF_V_DIM), BF16),
        compiler_params=_params(("parallel", "parallel", "arbitrary")),
        name="diff_attention_fixed",
    )(slopes_shift, q, kt, v, lam_rows, sub_g)


def _proj_res_kernel(x_ref, w_ref, h_ref, gate_ref, o_ref):
    y = jnp.dot(x_ref[0], w_ref[...], preferred_element_type=F32)
    o_ref[0] = h_ref[0] + gate_ref[0] * y


def _proj_res(x, w, h, gate, tl):
    bsz, seq, kdim = x.shape
    tok = lambda b, t: (b, t, 0)
    return pl.pallas_call(
        _proj_res_kernel,
        grid=(bsz, seq // tl),
        in_specs=[
            pl.BlockSpec((1, tl, kdim), tok),
            pl.BlockSpec((kdim, D_MODEL), lambda b, t: (0, 0)),
            pl.BlockSpec((1, tl, D_MODEL), tok),
            pl.BlockSpec((1, 1, D_MODEL), lambda b, t: (b, 0, 0)),
        ],
        out_specs=pl.BlockSpec((1, tl, D_MODEL), tok),
        out_shape=jax.ShapeDtypeStruct((bsz, seq, D_MODEL), F32),
        compiler_params=_params(("parallel", "parallel")),
        name="attn_out_proj",
    )(x, w, h, gate)


def _moe_route_kernel(h_ref, mod_ref, ng_ref, wr_ref, br_ref, u_ref, route_ref):
    mod = mod_ref[0]
    u = _mod_norm(h_ref[0], ng_ref[...], mod[0:1], mod[1:2])
    _store_row_slabs(u_ref.at[0], u)
    logits = jnp.dot(u.astype(BF16), wr_ref[...], preferred_element_type=F32) + br_ref[...]
    gl = logits[:, :LANES]
    el = logits[:, LANES:]
    lane = lax.broadcasted_iota(jnp.int32, gl.shape, 1)
    lane_f = lane.astype(F32)
    none = float(LANES)

    gmask = lane < N_GROUPS
    glm = jnp.where(gmask, gl, NEG_BIG)
    gmax = jnp.max(glm, axis=-1, keepdims=True)
    g_idx = jnp.min(jnp.where(glm == gmax, lane_f, none), axis=-1, keepdims=True)
    g_w = 1.0 / jnp.sum(jnp.where(gmask, jnp.exp(glm - gmax), 0.0), axis=-1, keepdims=True)

    lo = g_idx * EXPERTS_PER_GROUP
    emask = (lane_f >= lo) & (lane_f < lo + EXPERTS_PER_GROUP)
    elm = jnp.where(emask, el, NEG_BIG)
    emax = jnp.max(elm, axis=-1, keepdims=True)
    ee = jnp.where(emask, jnp.exp(elm - emax), 0.0)
    ep = ee / jnp.sum(ee, axis=-1, keepdims=True)
    epm = jnp.where(emask, ep, -1.0)
    p1 = jnp.max(epm, axis=-1, keepdims=True)
    i1 = jnp.min(jnp.where(epm == p1, lane_f, none), axis=-1, keepdims=True)
    epm2 = jnp.where(lane_f == i1, -1.0, epm)
    p2 = jnp.max(epm2, axis=-1, keepdims=True)
    i2 = jnp.min(jnp.where(epm2 == p2, lane_f, none), axis=-1, keepdims=True)
    denom = p1 + p2
    route_ref[0] = jnp.where(lane == 0, i1, jnp.where(lane == 1, i2, jnp.where(
        lane == 2, p1 / denom * g_w, jnp.where(lane == 3, p2 / denom * g_w, 0.0))))


def _moe_route(h, mod2, ng, wr, br, tl):
    bsz, seq, _ = h.shape
    tok = lambda b, t: (b, t, 0)
    const2 = lambda b, t: (0, 0)
    return pl.pallas_call(
        _moe_route_kernel,
        grid=(bsz, seq // tl),
        in_specs=[
            pl.BlockSpec((1, tl, D_MODEL), tok),
            pl.BlockSpec((1, 2, D_MODEL), lambda b, t: (b, 0, 0)),
            pl.BlockSpec((1, D_MODEL), const2),
            pl.BlockSpec((D_MODEL, 2 * LANES), const2),
            pl.BlockSpec((1, 2 * LANES), const2),
        ],
        out_specs=[pl.BlockSpec((1, tl, ROW_SLABS, LANES), lambda b, t: (b, t, 0, 0)),
                   pl.BlockSpec((1, tl, LANES), tok)],
        out_shape=[jax.ShapeDtypeStruct((bsz, seq, ROW_SLABS, LANES), F32),
                   jax.ShapeDtypeStruct((bsz, seq, LANES), F32)],
        compiler_params=_params(("parallel", "parallel")),
        name="moe_route",
    )(h, mod2, ng, wr, br)


def _moe_schedule(route, tm):
    t = route.shape[0]
    a = 2 * t
    n_tiles = a // tm + N_EXPERTS
    ea = route[:, :2].astype(jnp.int32).reshape(a)
    _, order = lax.sort((ea, jnp.arange(a, dtype=jnp.int32)), num_keys=1, is_stable=True)
    counts = jnp.sum((ea[:, None] == jnp.arange(N_EXPERTS, dtype=jnp.int32)[None, :]).astype(jnp.int32), axis=0)
    padded = (counts + tm - 1) // tm * tm
    pad_end = jnp.cumsum(padded)
    pad_start = pad_end - padded
    start = jnp.cumsum(counts) - counts
    tile_first = jnp.arange(n_tiles, dtype=jnp.int32) * tm
    tile_e = jnp.minimum(jnp.sum((tile_first[:, None] >= pad_end[None, :]).astype(jnp.int32), axis=1),
                         N_EXPERTS - 1)
    tile_valid = (tile_first < pad_end[-1]).astype(jnp.int32)
    row = jnp.arange(tm, dtype=jnp.int32)[None, :]
    local = (tile_first - pad_start[tile_e])[:, None] + row
    valid = local < counts[tile_e][:, None]
    asg = order[jnp.clip(start[tile_e][:, None] + local, 0, a - 1)]
    tok = jnp.where(valid, asg // 2, 0)
    dump = 2 * t + (jnp.arange(n_tiles, dtype=jnp.int32) % 2)[:, None] * tm + row
    dest = jnp.where(valid, (asg % 2) * t + asg // 2, dump)
    return tile_e, tile_valid, tok.reshape(n_tiles, 1, tm), dest.reshape(n_tiles, 1, tm)


def _moe_ffn_kernel(te_ref, tv_ref, tok_ref, tokn_ref, dest_ref, w1_ref, w3_ref, w2_ref, u_hbm, y_hbm,
                    xbuf, ybuf, w1b, w3b, w2b, gsem, ssem, *, tm, n_tiles):
    i = pl.program_id(0)
    slot = lax.rem(i, 2)

    def gather_rows(idx_ref, s):
        def body(r, c):
            pltpu.make_async_copy(u_hbm.at[pl.ds(idx_ref[0, 0, r], 1)], xbuf.at[s, pl.ds(r, 1)],
                                  gsem.at[s]).start()
            return c
        lax.fori_loop(0, tm, body, 0, unroll=8)

    def wait_gather(s):
        pltpu.make_async_copy(u_hbm.at[pl.ds(0, tm)], xbuf.at[s], gsem.at[s]).wait()

    def wait_scatter(s):
        pltpu.make_async_copy(ybuf.at[s], y_hbm.at[pl.ds(0, tm)], ssem.at[s]).wait()

    @pl.when(i == 0)
    def _():
        ybuf[...] = jnp.zeros(ybuf.shape, F32)
        n_real = y_hbm.shape[0] - 2 * tm
        for s in range(2):
            fill = pltpu.make_async_copy(ybuf.at[s], y_hbm.at[pl.ds(n_real + s * tm, tm)], ssem.at[s])
            fill.start()
            fill.wait()

    @pl.when((i == 0) & (tv_ref[0] == 1))
    def _():
        gather_rows(tok_ref, 0)

    @pl.when((i + 1 < n_tiles) & (tv_ref[jnp.minimum(i + 1, n_tiles - 1)] == 1))
    def _():
        gather_rows(tokn_ref, 1 - slot)

    @pl.when((i >= 2) & (tv_ref[jnp.maximum(i - 2, 0)] == 1))
    def _():
        wait_scatter(slot)

    @pl.when(tv_ref[i] == 1)
    def _():
        @pl.when((i == 0) | (te_ref[i] != te_ref[jnp.maximum(i - 1, 0)]))
        def _():
            w1b[...] = w1_ref[0, 0].astype(BF16)
            w3b[...] = w3_ref[0, 0].astype(BF16)
            w2b[...] = w2_ref[0, 0].astype(BF16)

        wait_gather(slot)
        x = _load_row_slabs(xbuf.at[slot]).astype(BF16)
        hid = _silu(jnp.dot(x, w1b[...], preferred_element_type=F32)) * jnp.dot(
            x, w3b[...], preferred_element_type=F32)
        _store_row_slabs(ybuf.at[slot], jnp.dot(hid.astype(BF16), w2b[...], preferred_element_type=F32))

        def body(r, c):
            pltpu.make_async_copy(ybuf.at[slot, pl.ds(r, 1)], y_hbm.at[pl.ds(dest_ref[0, 0, r], 1)],
                                  ssem.at[slot]).start()
            return c
        lax.fori_loop(0, tm, body, 0, unroll=8)

    @pl.when(i == n_tiles - 1)
    def _():
        @pl.when(tv_ref[n_tiles - 2] == 1)
        def _():
            wait_scatter(1 - slot)

        @pl.when(tv_ref[n_tiles - 1] == 1)
        def _():
            wait_scatter(slot)


def _moe_ffn(u2d, tile_e, tile_valid, tok, dest, w1, w3, w2, layer, tm):
    t = u2d.shape[0]
    n_tiles = tok.shape[0]
    smem_tile = lambda f: pl.BlockSpec((1, 1, tm), f, memory_space=pltpu.SMEM)
    wspec = lambda shp: pl.BlockSpec((1,) + shp, lambda i, te, tv: (layer, te[i], 0, 0))
    grid_spec = pltpu.PrefetchScalarGridSpec(
        num_scalar_prefetch=2,
        grid=(n_tiles,),
        in_specs=[
            smem_tile(lambda i, te, tv: (i, 0, 0)),
            smem_tile(lambda i, te, tv: (jnp.minimum(i + 1, n_tiles - 1), 0, 0)),
            smem_tile(lambda i, te, tv: (i, 0, 0)),
            wspec((1, D_MODEL, D_EXPERT)),
            wspec((1, D_MODEL, D_EXPERT)),
            wspec((1, D_EXPERT, D_MODEL)),
            pl.BlockSpec(memory_space=pl.ANY),
        ],
        out_specs=pl.BlockSpec(memory_space=pl.ANY),
        scratch_shapes=[
            pltpu.VMEM((2, tm, ROW_SLABS, LANES), F32),
            pltpu.VMEM((2, tm, ROW_SLABS, LANES), F32),
            pltpu.VMEM((D_MODEL, D_EXPERT), BF16),
            pltpu.VMEM((D_MODEL, D_EXPERT), BF16),
            pltpu.VMEM((D_EXPERT, D_MODEL), BF16),
            pltpu.SemaphoreType.DMA((2,)),
            pltpu.SemaphoreType.DMA((2,)),
        ],
    )
    return pl.pallas_call(
        functools.partial(_moe_ffn_kernel, tm=tm, n_tiles=n_tiles),
        grid_spec=grid_spec,
        out_shape=jax.ShapeDtypeStruct((2 * t + 2 * tm, ROW_SLABS, LANES), F32),
        compiler_params=_params(("arbitrary",)),
        name="moe_ffn",
    )(tile_e, tile_valid, tok, tok, dest, w1, w3, w2, u2d)


def _moe_combine_kernel(h_ref, gate_ref, route_ref, ya_ref, yb_ref, o_ref):
    route = route_ref[0]
    lane = lax.broadcasted_iota(jnp.int32, route.shape, 1)
    wa = jnp.sum(jnp.where(lane == 2, route, 0.0), axis=-1, keepdims=True)
    wb = jnp.sum(jnp.where(lane == 3, route, 0.0), axis=-1, keepdims=True)
    o_ref[0] = h_ref[0] + gate_ref[0] * (wa * _load_row_slabs(ya_ref) + wb * _load_row_slabs(yb_ref))


def _moe_combine(h, gate, route, y2, tl):
    bsz, seq, _ = h.shape
    nt = seq // tl
    tok = lambda b, t: (b, t, 0)
    return pl.pallas_call(
        _moe_combine_kernel,
        grid=(bsz, nt),
        in_specs=[
            pl.BlockSpec((1, tl, D_MODEL), tok),
            pl.BlockSpec((1, 1, D_MODEL), lambda b, t: (b, 0, 0)),
            pl.BlockSpec((1, tl, LANES), tok),
            pl.BlockSpec((tl, ROW_SLABS, LANES), lambda b, t: (b * nt + t, 0, 0)),
            pl.BlockSpec((tl, ROW_SLABS, LANES), lambda b, t: (bsz * nt + b * nt + t, 0, 0)),
        ],
        out_specs=pl.BlockSpec((1, tl, D_MODEL), tok),
        out_shape=jax.ShapeDtypeStruct((bsz, seq, D_MODEL), F32),
        compiler_params=_params(("parallel", "parallel")),
        name="moe_combine",
    )(h, gate, route, y2, y2)


def _alibi_slopes():
    start = 2.0 ** (-8.0 / DIFF_HEADS)
    return jnp.asarray([start ** (i + 1) for i in range(DIFF_HEADS)], F32)


def _head_expand_matrix():
    head_of_lane = jnp.arange(D_INNER) // SSM_HEAD_DIM
    e = (jnp.arange(DT_PAD)[:, None] == head_of_lane[None, :]).astype(BF16)
    return jnp.concatenate([e, e], axis=0)


def _pad_lanes(x, width):
    return jnp.pad(x, [(0, 0)] * (x.ndim - 1) + [(0, width - x.shape[-1])])


def _prep_weights(p):
    w = {}
    w_in = p["ssm_w_in"]
    w["ssm_w_in"] = jnp.concatenate(
        [w_in[..., :D_INNER + CONV_DIM], _pad_lanes(w_in[..., D_INNER + CONV_DIM:], DT_PAD)], axis=-1).astype(BF16)
    w["ssm_dt_bias"] = _pad_lanes(p["ssm_dt_bias"], DT_PAD)[:, None, :]
    w["ssm_a_log"] = _pad_lanes(p["ssm_a_log"], DT_PAD)[:, None, :]
    w["ssm_d"] = jnp.repeat(p["ssm_d"], SSM_HEAD_DIM, axis=-1)[:, None, :]
    w["ssm_w_out"] = p["ssm_w_out"].astype(BF16)
    w["e2"] = _head_expand_matrix()
    w["w_kv"] = p["w_kv"].astype(BF16)
    w["k_norm_g"] = jnp.tile(p["k_norm_g"], 2 * DIFF_HEADS)[None, :]
    w["attn_w_q"] = p["attn_w_q"].astype(BF16)
    w["attn_q_norm_g"] = jnp.tile(p["attn_q_norm_g"], (1, 2 * DIFF_HEADS))[:, None, :]
    lam = jnp.stack([p["attn_lambda_q1"], p["attn_lambda_k1"], p["attn_lambda_q2"], p["attn_lambda_k2"]], axis=1)
    w["lam_rows"] = jnp.pad(lam, ((0, 0), (0, 4), (0, LANES - DIFF_HEAD_DIM)))
    w["attn_w_o"] = p["attn_w_o"].astype(BF16)
    wr = jnp.concatenate([_pad_lanes(p["moe_w_group"], LANES), _pad_lanes(p["moe_w_expert"], LANES)], axis=-1)
    br = jnp.concatenate([_pad_lanes(p["moe_b_group"], LANES), _pad_lanes(p["moe_b_expert"], LANES)], axis=-1)
    w["moe_wr"] = wr.astype(BF16)
    w["moe_br"] = br[:, None, :]
    return w


def _trunk(x, mods, kv_mod, conv_prev, ssm_prev, k_past, v_past, p, w, cfg):
    bsz, seq, _ = x.shape
    tl, lt, lc, tq, tk = cfg["tl"], cfg["lt"], cfg["lc"], cfg["tq"], cfg["tk"]
    pos0 = 0 if k_past is None else k_past.shape[1]
    h = x
    conv_new, ssm_new = [], []
    k_new = v_new = k_all = kt_all = v_all = None
    for layer in range(DEPTH):
        mod = mods[layer]
        if layer < N_A_LAYERS:
            i = layer
            cprev = jnp.pad(conv_prev[i], ((0, 0), (8 - (CONV_WIDTH - 1), 0), (0, 0)))
            sprev = ssm_prev[i].reshape(bsz, D_INNER, SSM_STATE)
            h, cn, sn = _mamba_layer(
                h, mod, p["norm_g"][layer, 0][None, :], w["ssm_w_in"][i], p["ssm_conv_w"][i],
                p["ssm_conv_b"][i][None, :], w["ssm_dt_bias"][i], w["ssm_a_log"][i], w["ssm_d"][i],
                p["ssm_norm_g"][i][None, :], w["ssm_w_out"][i], cprev, sprev, w["e2"], lt, lc)
            conv_new.append(cn[:, 8 - (CONV_WIDTH - 1):, :])
            ssm_new.append(sn.reshape(bsz, SSM_HEADS, SSM_HEAD_DIM, SSM_STATE))
        else:
            j = layer - N_A_LAYERS
            lam_init = 0.8 - 0.6 * math.exp(-0.3 * layer)
            q = _normproj(h, mod[:, 0:2], p["norm_g"][layer, 0][None, :], w["attn_w_q"][j],
                          w["attn_q_norm_g"][j], tl, False, "attn_q_proj")[0]
            slopes = _alibi_slopes()
            sub_g = p["attn_sub_g"][j][None, :]
            online = functools.partial(_diff_attention, slopes=slopes, lam_rows=w["lam_rows"][j], sub_g=sub_g,
                                       tq=tq, tk=tk, pos0=pos0, lam_init=lam_init)
            if k_past is None and tq == tk:
                bound = (DIFF_HEAD_DIM ** 0.5 * (1.0 + 2.0 ** -6) * jnp.max(jnp.abs(p["attn_q_norm_g"][j]))
                         * jnp.max(jnp.abs(p["k_norm_g"])))
                slopes_shift = jnp.concatenate([slopes, bound[None], jnp.zeros((7,), F32)])
                o = lax.cond(
                    bound <= FIXED_SHIFT_LIMIT,
                    lambda q_, k_, kt_, v_: _diff_attention_fixed(q_, kt_, v_, slopes_shift, w["lam_rows"][j],
                                                                  sub_g, tq, lam_init),
                    lambda q_, k_, kt_, v_: online(q_, k_, v_),
                    q, k_all, kt_all, v_all)
            else:
                o = online(q, k_all, v_all)
            h = _proj_res(o, w["attn_w_o"][j], h, mod[:, 2:3], tl)
        u, route = _moe_route(h, mod[:, 3:5], p["norm_g"][layer, 1][None, :], w["moe_wr"][layer],
                              w["moe_br"][layer], tl)
        tile_e, tile_valid, tok, dest = _moe_schedule(route.reshape(bsz * seq, LANES), cfg["tm"])
        y2 = _moe_ffn(u.reshape(bsz * seq, ROW_SLABS, LANES), tile_e, tile_valid, tok, dest, p["moe_w1"],
                      p["moe_w3"], p["moe_w2"], layer, cfg["tm"])
        h = _moe_combine(h, mod[:, 5:6], route, y2, tl)
        if layer == N_A_LAYERS - 1:
            k_f32, v_f32, k_all, v_all = _normproj(h, kv_mod, p["kv_norm_g"][None, :], w["w_kv"], w["k_norm_g"],
                                                   tl, True, "shared_kv")
            nk_cols = DIFF_HEADS * 2 * DIFF_HEAD_DIM
            k_new = k_f32.reshape(bsz, seq, DIFF_HEADS, 2, DIFF_HEAD_DIM)
            v_new = v_f32.reshape(bsz, seq, DIFF_HEADS, DIFF_V_DIM)
            if k_past is None and tq == tk:
                kt_all = _key_blocks_transposed(k_all, tk)
            if k_past is not None:
                k_all = jnp.concatenate([k_past.reshape(bsz, pos0, nk_cols).astype(BF16), k_all], axis=1)
                v_all = jnp.concatenate([v_past.reshape(bsz, pos0, nk_cols).astype(BF16), v_all], axis=1)
                pad = (-k_all.shape[1]) % tk
                k_all = jnp.pad(k_all, ((0, 0), (0, pad), (0, 0)))
                v_all = jnp.pad(v_all, ((0, 0), (0, pad), (0, 0)))
    return h, jnp.stack(conv_new), jnp.stack(ssm_new), k_new, v_new


PROMPT_CFG = dict(tl=512, lt=256, lc=64, tq=512, tk=512, tm=256)
SAMPLE_CFG = dict(tl=64, lt=64, lc=64, tq=64, tk=1152, tm=64)


def kernel(x_prompt, x_sample, c_prompt, c_sample, state_conv, state_ssm, cache_k, cache_v, ada_w, ada_b, norm_g, ssm_w_in, ssm_conv_w, ssm_conv_b, ssm_dt_bias, ssm_a_log, ssm_d, ssm_norm_g, ssm_w_out, kv_norm_g, kv_ada_w, kv_ada_b, w_kv, k_norm_g, attn_w_q, attn_q_norm_g, attn_lambda_q1, attn_lambda_k1, attn_lambda_q2, attn_lambda_k2, attn_sub_g, attn_w_o, moe_w_group, moe_b_group, moe_w_expert, moe_b_expert, moe_w1, moe_w3, moe_w2):
    p = dict(norm_g=norm_g, ssm_w_in=ssm_w_in, ssm_conv_w=ssm_conv_w, ssm_conv_b=ssm_conv_b,
             ssm_dt_bias=ssm_dt_bias, ssm_a_log=ssm_a_log, ssm_d=ssm_d, ssm_norm_g=ssm_norm_g,
             ssm_w_out=ssm_w_out, kv_norm_g=kv_norm_g, w_kv=w_kv, k_norm_g=k_norm_g, attn_w_q=attn_w_q,
             attn_q_norm_g=attn_q_norm_g, attn_lambda_q1=attn_lambda_q1, attn_lambda_k1=attn_lambda_k1,
             attn_lambda_q2=attn_lambda_q2, attn_lambda_k2=attn_lambda_k2, attn_sub_g=attn_sub_g,
             attn_w_o=attn_w_o, moe_w_group=moe_w_group, moe_b_group=moe_b_group, moe_w_expert=moe_w_expert,
             moe_b_expert=moe_b_expert, moe_w1=moe_w1, moe_w3=moe_w3, moe_w2=moe_w2)
    w = _prep_weights(p)

    bp, bs = x_prompt.shape[0], x_sample.shape[0]
    c_all = jnp.concatenate([c_prompt, c_sample], axis=0)
    c_all = jnp.pad(c_all, ((0, (-c_all.shape[0]) % 8), (0, 0)))
    mods = _mods(c_all, ada_w, ada_b[:, None, :], 1536).reshape(DEPTH, -1, 6, D_MODEL)
    kv_mods = _mods(c_all, kv_ada_w[None], kv_ada_b[None, None, :], 1024).reshape(-1, 2, D_MODEL)

    conv0 = jnp.zeros((N_A_LAYERS, bp, CONV_WIDTH - 1, CONV_DIM), F32)
    ssm0 = jnp.zeros((N_A_LAYERS, bp, SSM_HEADS, SSM_HEAD_DIM, SSM_STATE), F32)
    y_p, conv_p, ssm_p, k_p, v_p = _trunk(x_prompt, mods[:, :bp], kv_mods[:bp], conv0, ssm0, None, None,
                                          p, w, PROMPT_CFG)
    y_s, conv_s, ssm_s, k_s, v_s = _trunk(x_sample, mods[:, bp:bp + bs], kv_mods[bp:bp + bs], state_conv,
                                          state_ssm, cache_k, cache_v, p, w, SAMPLE_CFG)
    return (y_p, y_s, conv_p, ssm_p, k_p, v_p, conv_s, ssm_s, k_s, v_s)
```

```python
import functools
import math

import jax
import jax.numpy as jnp
from jax import lax
from jax.experimental import pallas as pl
from jax.experimental.pallas import tpu as pltpu

F32 = jnp.float32
BF16 = jnp.bfloat16
HIGHEST = lax.Precision.HIGHEST

D_MODEL = 1024
DEPTH = 4
N_A_LAYERS = 2
EPS = 1e-6
CHUNK = 64
D_INNER = 2048
SSM_HEADS = 32
SSM_HEAD_DIM = 64
SSM_GROUPS = 4
SSM_STATE = 128
GROUP_WIDTH = D_INNER // SSM_GROUPS
CONV_WIDTH = 4
CONV_DIM = D_INNER + 2 * SSM_GROUPS * SSM_STATE
DT_PAD = 128
IN_PROJ_PAD = D_INNER + CONV_DIM + DT_PAD
DIFF_HEADS = 8
DIFF_HEAD_DIM = 64
DIFF_V_DIM = 128
N_GROUPS = 4
EXPERTS_PER_GROUP = 8
N_EXPERTS = 32
D_EXPERT = 512

LANES = 128
ROW_SLABS = D_MODEL // LANES
NEG_BIG = -1e30
FIXED_SHIFT_LIMIT = 40.0
VMEM_LIMIT = 56 * 1024 * 1024


def _params(semantics, vmem=VMEM_LIMIT):
    return pltpu.CompilerParams(dimension_semantics=semantics, vmem_limit_bytes=vmem)


def _silu(x):
    return x * (1.0 / (1.0 + jnp.exp(-x)))


def _softplus(x):
    return jnp.maximum(x, 0.0) + jnp.log1p(jnp.exp(-jnp.abs(x)))


def _mod_norm(h, gain, shift, scale):
    ms = jnp.mean(h * h, axis=-1, keepdims=True)
    return (h * lax.rsqrt(ms + EPS) * gain) * (1.0 + scale) + shift


def _store_row_slabs(ref, x):
    for s in range(ROW_SLABS):
        ref[:, s, :] = x[:, s * LANES:(s + 1) * LANES]


def _load_row_slabs(ref):
    return jnp.concatenate([ref[:, s, :] for s in range(ROW_SLABS)], axis=1)


def _seg_rms64(x, g, n_cols):
    lane = lax.broadcasted_iota(jnp.int32, (x.shape[0], LANES), 1)
    low = lane < 64
    outs = []
    for cb in range(n_cols // LANES):
        xc = x[:, cb * LANES:(cb + 1) * LANES]
        sq = xc * xc
        s_lo = jnp.sum(jnp.where(low, sq, 0.0), axis=-1, keepdims=True)
        s_hi = jnp.sum(jnp.where(low, 0.0, sq), axis=-1, keepdims=True)
        r = jnp.where(low, lax.rsqrt(s_lo * (1.0 / 64) + EPS), lax.rsqrt(s_hi * (1.0 / 64) + EPS))
        outs.append(xc * r * g[:, cb * LANES:(cb + 1) * LANES])
    if n_cols < x.shape[1]:
        outs.append(x[:, n_cols:])
    return jnp.concatenate(outs, axis=1)


def _mods_kernel(c_ref, w_ref, b_ref, o_ref):
    s = _silu(c_ref[...])
    o_ref[0] = jnp.dot(s.astype(BF16), w_ref[0].astype(BF16), preferred_element_type=F32) + b_ref[0]


def _mods(c_all, w, b, tn):
    nl, _, n = w.shape
    m = c_all.shape[0]
    return pl.pallas_call(
        _mods_kernel,
        grid=(nl, n // tn),
        in_specs=[
            pl.BlockSpec((m, D_MODEL), lambda l, j: (0, 0)),
            pl.BlockSpec((1, D_MODEL, tn), lambda l, j: (l, 0, j)),
            pl.BlockSpec((1, 1, tn), lambda l, j: (l, 0, j)),
        ],
        out_specs=pl.BlockSpec((1, m, tn), lambda l, j: (l, 0, j)),
        out_shape=jax.ShapeDtypeStruct((nl, m, n), F32),
        compiler_params=_params(("parallel", "parallel")),
        name="adaln_mods",
    )(c_all, w, b)


def _expand_heads(v, e2_ref):
    hi = v.astype(BF16)
    lo = (v - hi.astype(F32)).astype(BF16)
    return jnp.dot(jnp.concatenate([hi, lo], axis=1), e2_ref[...], preferred_element_type=F32)


def _mamba_kernel(h_ref, mod_ref, ng_ref, w_ref, cw_ref, cb_ref, dtb_ref, alog_ref, dsk_ref, sng_ref,
                  wo_ref, cprev_ref, sprev_ref, e2_ref,
                  hout_ref, cnew_ref, snew_ref,
                  xbc_scr, st_scr, xd_scr, y_scr, *, lt, lc, nc):
    c = pl.program_id(1)

    @pl.when(c == 0)
    def _():
        xbc_scr[...] = cprev_ref[0]
        st_scr[...] = sprev_ref[0].T

    h = h_ref[0]
    mod = mod_ref[0]
    u = _mod_norm(h, ng_ref[...], mod[0:1], mod[1:2])
    proj = jnp.dot(u.astype(BF16), w_ref[...], preferred_element_type=F32)
    z = proj[:, :D_INNER]
    xbc_raw = proj[:, D_INNER:D_INNER + CONV_DIM]
    dt_raw = proj[:, D_INNER + CONV_DIM:]

    cw = cw_ref[...]
    prev8 = xbc_scr[...]
    row8 = lax.broadcasted_iota(jnp.int32, (8, CONV_DIM), 0)
    conv = cb_ref[...] + cw[CONV_WIDTH - 1:CONV_WIDTH] * xbc_raw
    for s in range(1, CONV_WIDTH):
        rolled = pltpu.roll(xbc_raw, s, 0)
        first8 = jnp.where(row8 < s, pltpu.roll(prev8, s, 0), rolled[0:8])
        tap = CONV_WIDTH - 1 - s
        conv = conv + cw[tap:tap + 1] * jnp.concatenate([first8, rolled[8:]], axis=0)
    tail = xbc_raw[lt - 8:lt]
    cnew_ref[0] = tail
    xbc_scr[...] = tail
    xbc = _silu(conv)
    xs = xbc[:, :D_INNER]
    bm = xbc[:, D_INNER:D_INNER + GROUP_WIDTH]
    cm = xbc[:, D_INNER + GROUP_WIDTH:].astype(BF16)

    dt = _softplus(dt_raw + dtb_ref[...])
    da = dt * (-jnp.exp(alog_ref[...]))
    xd_scr[...] = xs * _expand_heads(dt, e2_ref)

    row_i = lax.broadcasted_iota(jnp.int32, (lc, lc), 0)
    col_j = lax.broadcasted_iota(jnp.int32, (lc, lc), 1)
    tril = col_j <= row_i
    trilf = tril.astype(F32)
    low = lax.broadcasted_iota(jnp.int32, (lc, LANES), 1) < SSM_HEAD_DIM

    for s in range(lt // lc):
        r0 = s * lc
        acum = jnp.dot(trilf, da[r0:r0 + lc], precision=HIGHEST, preferred_element_type=F32)
        acum_t = acum.T
        acum_e = _expand_heads(acum, e2_ref)
        ea = jnp.exp(acum_e)
        dend = jnp.exp(acum_e[lc - 1:lc] - acum_e)
        xd_s = xd_scr[r0:r0 + lc, :]
        xdb = xd_s.astype(BF16)
        xdw = (xd_s * dend).astype(BF16)
        for g in range(SSM_GROUPS):
            gl = slice(g * GROUP_WIDTH, (g + 1) * GROUP_WIDTH)
            b_g = bm[r0:r0 + lc, g * SSM_STATE:(g + 1) * SSM_STATE]
            c_g = cm[r0:r0 + lc, g * SSM_STATE:(g + 1) * SSM_STATE]
            cbm = lax.dot_general(c_g, b_g.astype(BF16), (((1,), (1,)), ((), ())),
                                  preferred_element_type=F32)
            s_g = st_scr[:, gl]
            yoff = jnp.dot(c_g, s_g.astype(BF16), preferred_element_type=F32) * ea[:, gl]
            for j in range(GROUP_WIDTH // LANES):
                pair = g * (GROUP_WIDTH // LANES) + j
                pl_ = slice(pair * LANES, (pair + 1) * LANES)
                ys = []
                for hh in (2 * pair, 2 * pair + 1):
                    seg = acum[:, hh:hh + 1] - acum_t[hh:hh + 1, :]
                    dec = jnp.exp(jnp.where(tril, seg, NEG_BIG))
                    ys.append(jnp.dot((cbm * dec).astype(BF16), xdb[:, pl_], preferred_element_type=F32))
                y_scr[r0:r0 + lc, pl_] = jnp.where(low, ys[0], ys[1]) + yoff[:, j * LANES:(j + 1) * LANES]
            upd = jnp.dot(b_g.T.astype(BF16), xdw[:, gl], preferred_element_type=F32)
            st_scr[:, gl] = s_g * ea[lc - 1:lc, gl] + upd

    y = (y_scr[...] + dsk_ref[...] * xs) * _silu(z)
    parts = []
    for g in range(SSM_GROUPS):
        gl = slice(g * GROUP_WIDTH, (g + 1) * GROUP_WIDTH)
        yg = y[:, gl]
        ms = jnp.mean(yg * yg, axis=-1, keepdims=True)
        parts.append(yg * lax.rsqrt(ms + EPS) * sng_ref[:, gl])
    yn = jnp.concatenate(parts, axis=1).astype(BF16)
    out = jnp.dot(yn, wo_ref[...], preferred_element_type=F32)
    hout_ref[0] = h + mod[2:3] * out

    @pl.when(c == nc - 1)
    def _():
        snew_ref[0] = st_scr[...].T


def _mamba_layer(h, mod, ng, w_in, cw, cb, dtb, alog, dsk, sng, wo, cprev, sprev, e2, lt, lc):
    bsz, seq, _ = h.shape
    nc = seq // lt
    const2 = lambda b, c: (0, 0)
    perb = lambda b, c: (b, 0, 0)
    kern = functools.partial(_mamba_kernel, lt=lt, lc=lc, nc=nc)
    return pl.pallas_call(
        kern,
        grid=(bsz, nc),
        in_specs=[
            pl.BlockSpec((1, lt, D_MODEL), lambda b, c: (b, c, 0)),
            pl.BlockSpec((1, 6, D_MODEL), perb),
            pl.BlockSpec((1, D_MODEL), const2),
            pl.BlockSpec((D_MODEL, IN_PROJ_PAD), const2),
            pl.BlockSpec((CONV_WIDTH, CONV_DIM), const2),
            pl.BlockSpec((1, CONV_DIM), const2),
            pl.BlockSpec((1, DT_PAD), const2),
            pl.BlockSpec((1, DT_PAD), const2),
            pl.BlockSpec((1, D_INNER), const2),
            pl.BlockSpec((1, D_INNER), const2),
            pl.BlockSpec((D_INNER, D_MODEL), const2),
            pl.BlockSpec((1, 8, CONV_DIM), perb),
            pl.BlockSpec((1, D_INNER, SSM_STATE), perb),
            pl.BlockSpec((2 * DT_PAD, D_INNER), const2),
        ],
        out_specs=[
            pl.BlockSpec((1, lt, D_MODEL), lambda b, c: (b, c, 0)),
            pl.BlockSpec((1, 8, CONV_DIM), perb),
            pl.BlockSpec((1, D_INNER, SSM_STATE), perb),
        ],
        out_shape=[
            jax.ShapeDtypeStruct((bsz, seq, D_MODEL), F32),
            jax.ShapeDtypeStruct((bsz, 8, CONV_DIM), F32),
            jax.ShapeDtypeStruct((bsz, D_INNER, SSM_STATE), F32),
        ],
        scratch_shapes=[
            pltpu.VMEM((8, CONV_DIM), F32),
            pltpu.VMEM((SSM_STATE, D_INNER), F32),
            pltpu.VMEM((lt, D_INNER), F32),
            pltpu.VMEM((lt, D_INNER), F32),
        ],
        compiler_params=_params(("parallel", "arbitrary")),
        name="mamba_layer",
    )(h, mod, ng, w_in, cw, cb, dtb, alog, dsk, sng, wo, cprev, sprev, e2)


def _normproj_kernel(h_ref, mod_ref, ng_ref, w_ref, sg_ref, *out_refs, n_norm, want_f32):
    mod = mod_ref[0]
    u = _mod_norm(h_ref[0], ng_ref[...], mod[0:1], mod[1:2])
    y = jnp.dot(u.astype(BF16), w_ref[...], preferred_element_type=F32)
    y = _seg_rms64(y, sg_ref[...], n_norm)
    n_parts = y.shape[1] // D_MODEL
    for part in range(n_parts):
        yp = y[:, part * D_MODEL:(part + 1) * D_MODEL]
        if want_f32:
            out_refs[part][0] = yp
        out_refs[part - n_parts][0] = yp.astype(BF16)


def _normproj(h, mod2, ng, w, seg_g, tl, want_f32, name):
    bsz, seq, _ = h.shape
    nout = w.shape[1]
    n_norm = seg_g.shape[1]
    n_parts = nout // D_MODEL
    tok = lambda b, t: (b, t, 0)
    const2 = lambda b, t: (0, 0)
    out_specs = [pl.BlockSpec((1, tl, D_MODEL), tok)] * n_parts
    out_shape = [jax.ShapeDtypeStruct((bsz, seq, D_MODEL), BF16)] * n_parts
    if want_f32:
        out_specs = [pl.BlockSpec((1, tl, D_MODEL), tok)] * n_parts + out_specs
        out_shape = [jax.ShapeDtypeStruct((bsz, seq, D_MODEL), F32)] * n_parts + out_shape
    return pl.pallas_call(
        functools.partial(_normproj_kernel, n_norm=n_norm, want_f32=want_f32),
        grid=(bsz, seq // tl),
        in_specs=[
            pl.BlockSpec((1, tl, D_MODEL), tok),
            pl.BlockSpec((1, 2, D_MODEL), lambda b, t: (b, 0, 0)),
            pl.BlockSpec((1, D_MODEL), const2),
            pl.BlockSpec((D_MODEL, nout), const2),
            pl.BlockSpec((1, n_norm), const2),
        ],
        out_specs=out_specs,
        out_shape=out_shape,
        compiler_params=_params(("parallel", "parallel")),
        name=name,
    )(h, mod2, ng, w, seg_g)


def _attn_kernel(slope_ref, q_ref, k_ref, v_ref, lam_ref, sg_ref, o_ref,
                 m1, l1, a1, m2, l2, a2, *, tq, tk, nk, pos0, lam_init):
    hd = pl.program_id(1)
    qi = pl.program_id(2)
    ki = pl.program_id(3)

    @pl.when(ki == 0)
    def _():
        for m, l, a in ((m1, l1, a1), (m2, l2, a2)):
            m[...] = jnp.full(m.shape, NEG_BIG, F32)
            l[...] = jnp.zeros(l.shape, F32)
            a[...] = jnp.zeros(a.shape, F32)

    q_first = pos0 + qi * tq
    last_q_chunk = (q_first + tq - 1) // CHUNK
    first_k_chunk = (ki * tk) // CHUNK

    @pl.when(first_k_chunk <= last_q_chunk)
    def _():
        q = q_ref[0]
        k = k_ref[0]
        v = v_ref[0]
        lane = lax.broadcasted_iota(jnp.int32, q.shape, 1)
        zero = jnp.zeros_like(q)
        qa = jnp.where(lane < DIFF_HEAD_DIM, q, zero)
        qb = jnp.where(lane < DIFF_HEAD_DIM, zero, q)
        qpos = q_first + lax.broadcasted_iota(jnp.int32, (tq, tk), 0)
        kpos = ki * tk + lax.broadcasted_iota(jnp.int32, (tq, tk), 1)
        visible = (kpos // CHUNK) <= (qpos // CHUNK)
        dist = jnp.abs(qpos - kpos).astype(F32)
        bias = jnp.where(visible, -slope_ref[hd] * dist, NEG_BIG)
        scale = DIFF_HEAD_DIM ** -0.5
        nt = (((1,), (1,)), ((), ()))
        for qq, m, l, a in ((qa, m1, l1, a1), (qb, m2, l2, a2)):
            s = lax.dot_general(qq, k, nt, preferred_element_type=F32) * scale + bias
            m_old = m[...]
            m_new = jnp.maximum(m_old, jnp.max(s, axis=-1, keepdims=True))
            p = jnp.exp(s - m_new)
            alpha = jnp.exp(m_old - m_new)
            l[...] = alpha * l[...] + jnp.sum(p, axis=-1, keepdims=True)
            a[...] = alpha * a[...] + jnp.dot(p.astype(BF16), v, preferred_element_type=F32)
            m[...] = m_new

    @pl.when(ki == nk - 1)
    def _():
        lp = lam_ref[...]
        lam = (jnp.exp(jnp.sum(lp[0:1] * lp[1:2], axis=-1, keepdims=True))
               - jnp.exp(jnp.sum(lp[2:3] * lp[3:4], axis=-1, keepdims=True)) + lam_init)
        o = a1[...] / l1[...] - lam * (a2[...] / l2[...])
        ms = jnp.mean(o * o, axis=-1, keepdims=True)
        o_ref[0] = (o * lax.rsqrt(ms + EPS) * sg_ref[...] * (1.0 - lam_init)).astype(BF16)


def _diff_attention(q, k, v, slopes, lam_rows, sub_g, tq, tk, pos0, lam_init):
    bsz, lq, _ = q.shape
    lk = k.shape[1]
    nq, nk = lq // tq, lk // tk

    def kv_map(b, h, i, j, slope_ref):
        last_visible = ((pos0 + i * tq + tq - 1) // CHUNK * CHUNK + CHUNK - 1) // tk
        return (b, jnp.minimum(j, last_visible), h)

    grid_spec = pltpu.PrefetchScalarGridSpec(
        num_scalar_prefetch=1,
        grid=(bsz, DIFF_HEADS, nq, nk),
        in_specs=[
            pl.BlockSpec((1, tq, LANES), lambda b, h, i, j, s: (b, i, h)),
            pl.BlockSpec((1, tk, LANES), kv_map),
            pl.BlockSpec((1, tk, LANES), kv_map),
            pl.BlockSpec((8, LANES), lambda b, h, i, j, s: (0, 0)),
            pl.BlockSpec((1, LANES), lambda b, h, i, j, s: (0, 0)),
        ],
        out_specs=pl.BlockSpec((1, tq, LANES), lambda b, h, i, j, s: (b, i, h)),
        scratch_shapes=[
            pltpu.VMEM((tq, 1), F32), pltpu.VMEM((tq, 1), F32), pltpu.VMEM((tq, LANES), F32),
            pltpu.VMEM((tq, 1), F32), pltpu.VMEM((tq, 1), F32), pltpu.VMEM((tq, LANES), F32),
        ],
    )
    return pl.pallas_call(
        functools.partial(_attn_kernel, tq=tq, tk=tk, nk=nk, pos0=pos0, lam_init=lam_init),
        grid_spec=grid_spec,
        out_shape=jax.ShapeDtypeStruct((bsz, lq, DIFF_HEADS * DIFF_V_DIM), BF16),
        compiler_params=_params(("parallel", "parallel", "parallel", "arbitrary")),
        name="diff_attention",
    )(slopes, q, k, v, lam_rows, sub_g)


def _attn_fixed_kernel(sc_ref, q_ref, kt_ref, v_ref, lam_ref, sg_ref, o_ref, acc1, acc2, *, t, lam_init):
    hd = pl.program_id(1)
    qi = pl.program_id(2)
    slope = sc_ref[hd]
    shift = sc_ref[DIFF_HEADS]

    lane = lax.broadcasted_iota(jnp.int32, (t, LANES), 1)
    qoff = lax.broadcasted_iota(jnp.int32, (t, LANES), 0)
    q_hi = (qoff >> 8).astype(F32) * 256.0
    q_lo = (qoff & 255).astype(F32)
    aug_q = jnp.where(lane == 0, -slope * q_hi, jnp.where(lane == 1, -slope * q_lo, jnp.where(
        (lane == 2) | (lane == 3), 1.0, jnp.where(lane == 4, -shift, 0.0)))).astype(BF16)
    qs = q_ref[0] * jnp.asarray(DIFF_HEAD_DIM ** -0.5, BF16)
    zero = jnp.zeros_like(qs)
    qa = jnp.concatenate([jnp.where(lane < DIFF_HEAD_DIM, qs, zero), aug_q], axis=1)
    qb = jnp.concatenate([jnp.where(lane < DIFF_HEAD_DIM, zero, qs), aug_q], axis=1)
    acc1[...] = jnp.zeros(acc1.shape, F32)
    acc2[...] = jnp.zeros(acc2.shape, F32)
    ones = jnp.ones((t, LANES), BF16)
    aug_row = lax.broadcasted_iota(jnp.int32, (16, t), 0)
    koff = lax.broadcasted_iota(jnp.int32, (16, t), 1)
    pad_rows = jnp.zeros((LANES - 16, t), BF16)

    def accumulate(kb, aug_k, bias):
        k0 = pl.multiple_of(kb * t, t)
        kfull = jnp.concatenate([kt_ref[0, 0, kb], aug_k.astype(BF16), pad_rows], axis=0)
        v2 = jnp.concatenate([v_ref[0, pl.ds(k0, t), :], ones], axis=1)
        for qq, acc in ((qa, acc1), (qb, acc2)):
            s = jnp.dot(qq, kfull, preferred_element_type=F32)
            if bias is not None:
                s = s + bias
            acc[...] += jnp.dot(jnp.exp(s).astype(BF16), v2, preferred_element_type=F32)

    def below_diagonal(kb, carry):
        rel = (kb - qi) * t + koff
        k_hi = (rel >> 8).astype(F32) * 256.0
        k_lo = (rel & 255).astype(F32)
        aug_k = jnp.where(aug_row <= 1, 1.0, jnp.where(aug_row == 2, slope * k_hi, jnp.where(
            aug_row == 3, slope * k_lo, jnp.where(aug_row == 4, 1.0, 0.0))))
        accumulate(kb, aug_k, None)
        return carry

    lax.fori_loop(0, qi, below_diagonal, 0)

    half = t // 2
    qo = lax.broadcasted_iota(jnp.int32, (half, half), 0)
    ko = lax.broadcasted_iota(jnp.int32, (half, half), 1)
    visible = (ko // CHUNK) <= (qo // CHUNK)
    bias_diag = jnp.where(visible, -slope * jnp.abs(qo - ko).astype(F32), NEG_BIG)
    bias_below = -slope * (qo + half - ko).astype(F32)
    kd0 = pl.multiple_of(qi * t, t)
    kfull_d = jnp.concatenate([kt_ref[0, 0, qi], jnp.where(aug_row == 4, 1.0, 0.0).astype(BF16), pad_rows], axis=0)
    v2_d = jnp.concatenate([v_ref[0, pl.ds(kd0, t), :], ones], axis=1)
    for r0, c0, bias in ((0, 0, bias_diag), (half, 0, bias_below), (half, half, bias_diag)):
        for qq, acc in ((qa, acc1), (qb, acc2)):
            s = jnp.dot(qq[r0:r0 + half], kfull_d[:, c0:c0 + half], preferred_element_type=F32) + bias
            acc[r0:r0 + half, :] += jnp.dot(jnp.exp(s).astype(BF16), v2_d[c0:c0 + half],
                                            preferred_element_type=F32)

    lp = lam_ref[...]
    lam = (jnp.exp(jnp.sum(lp[0:1] * lp[1:2], axis=-1, keepdims=True))
           - jnp.exp(jnp.sum(lp[2:3] * lp[3:4], axis=-1, keepdims=True)) + lam_init)
    a1 = acc1[...]
    a2 = acc2[...]
    o = a1[:, :LANES] / a1[:, LANES:] - lam * (a2[:, :LANES] / a2[:, LANES:])
    ms = jnp.mean(o * o, axis=-1, keepdims=True)
    o_ref[0] = (o * lax.rsqrt(ms + EPS) * sg_ref[...] * (1.0 - lam_init)).astype(BF16)


def _key_blocks_transposed(k, t):
    bsz, seq, _ = k.shape
    return k.reshape(bsz, seq // t, t, DIFF_HEADS, LANES).transpose(0, 3, 1, 4, 2)


def _diff_attention_fixed(q, kt, v, slopes_shift, lam_rows, sub_g, t, lam_init):
    bsz, seq, _ = q.shape
    nb = seq // t
    grid_spec = pltpu.PrefetchScalarGridSpec(
        num_scalar_prefetch=1,
        grid=(bsz, DIFF_HEADS, nb),
        in_specs=[
            pl.BlockSpec((1, t, LANES), lambda b, h, i, s: (b, i, h)),
            pl.BlockSpec((1, 1, nb, LANES, t), lambda b, h, i, s: (b, h, 0, 0, 0)),
            pl.BlockSpec((1, seq, LANES), lambda b, h, i, s: (b, 0, h)),
            pl.BlockSpec((8, LANES), lambda b, h, i, s: (0, 0)),
            pl.BlockSpec((1, LANES), lambda b, h, i, s: (0, 0)),
        ],
        out_specs=pl.BlockSpec((1, t, LANES), lambda b, h, i, s: (b, i, h)),
        scratch_shapes=[pltpu.VMEM((t, 2 * LANES), F32), pltpu.VMEM((t, 2 * LANES), F32)],
    )
    return pl.pallas_call(
        functools.partial(_attn_fixed_kernel, t=t, lam_init=lam_init),
        grid_spec=grid_spec,
        out_shape=jax.ShapeDtypeStruct((bsz, seq, DIFF_HEADS * DIFF_V_DIM), BF16),
        compiler_params=_params(("parallel", "parallel", "arbitrary")),
        name="diff_attention_fixed",
    )(slopes_shift, q, kt, v, lam_rows, sub_g)


def _proj_res_kernel(x_ref, w_ref, h_ref, gate_ref, o_ref):
    y = jnp.dot(x_ref[0], w_ref[...], preferred_element_type=F32)
    o_ref[0] = h_ref[0] + gate_ref[0] * y


def _proj_res(x, w, h, gate, tl):
    bsz, seq, kdim = x.shape
    tok = lambda b, t: (b, t, 0)
    return pl.pallas_call(
        _proj_res_kernel,
        grid=(bsz, seq // tl),
        in_specs=[
            pl.BlockSpec((1, tl, kdim), tok),
            pl.BlockSpec((kdim, D_MODEL), lambda b, t: (0, 0)),
            pl.BlockSpec((1, tl, D_MODEL), tok),
            pl.BlockSpec((1, 1, D_MODEL), lambda b, t: (b, 0, 0)),
        ],
        out_specs=pl.BlockSpec((1, tl, D_MODEL), tok),
        out_shape=jax.ShapeDtypeStruct((bsz, seq, D_MODEL), F32),
        compiler_params=_params(("parallel", "parallel")),
        name="attn_out_proj",
    )(x, w, h, gate)


def _moe_route_kernel(h_ref, mod_ref, ng_ref, wr_ref, br_ref, u_ref, route_ref):
    mod = mod_ref[0]
    u = _mod_norm(h_ref[0], ng_ref[...], mod[0:1], mod[1:2])
    _store_row_slabs(u_ref.at[0], u)
    logits = jnp.dot(u.astype(BF16), wr_ref[...], preferred_element_type=F32) + br_ref[...]
    gl = logits[:, :LANES]
    el = logits[:, LANES:]
    lane = lax.broadcasted_iota(jnp.int32, gl.shape, 1)
    lane_f = lane.astype(F32)
    none = float(LANES)

    gmask = lane < N_GROUPS
    glm = jnp.where(gmask, gl, NEG_BIG)
    gmax = jnp.max(glm, axis=-1, keepdims=True)
    g_idx = jnp.min(jnp.where(glm == gmax, lane_f, none), axis=-1, keepdims=True)
    g_w = 1.0 / jnp.sum(jnp.where(gmask, jnp.exp(glm - gmax), 0.0), axis=-1, keepdims=True)

    lo = g_idx * EXPERTS_PER_GROUP
    emask = (lane_f >= lo) & (lane_f < lo + EXPERTS_PER_GROUP)
    elm = jnp.where(emask, el, NEG_BIG)
    emax = jnp.max(elm, axis=-1, keepdims=True)
    ee = jnp.where(emask, jnp.exp(elm - emax), 0.0)
    ep = ee / jnp.sum(ee, axis=-1, keepdims=True)
    epm = jnp.where(emask, ep, -1.0)
    p1 = jnp.max(epm, axis=-1, keepdims=True)
    i1 = jnp.min(jnp.where(epm == p1, lane_f, none), axis=-1, keepdims=True)
    epm2 = jnp.where(lane_f == i1, -1.0, epm)
    p2 = jnp.max(epm2, axis=-1, keepdims=True)
    i2 = jnp.min(jnp.where(epm2 == p2, lane_f, none), axis=-1, keepdims=True)
    denom = p1 + p2
    route_ref[0] = jnp.where(lane == 0, i1, jnp.where(lane == 1, i2, jnp.where(
        lane == 2, p1 / denom * g_w, jnp.where(lane == 3, p2 / denom * g_w, 0.0))))


def _moe_route(h, mod2, ng, wr, br, tl):
    bsz, seq, _ = h.shape
    tok = lambda b, t: (b, t, 0)
    const2 = lambda b, t: (0, 0)
    return pl.pallas_call(
        _moe_route_kernel,
        grid=(bsz, seq // tl),
        in_specs=[
            pl.BlockSpec((1, tl, D_MODEL), tok),
            pl.BlockSpec((1, 2, D_MODEL), lambda b, t: (b, 0, 0)),
            pl.BlockSpec((1, D_MODEL), const2),
            pl.BlockSpec((D_MODEL, 2 * LANES), const2),
            pl.BlockSpec((1, 2 * LANES), const2),
        ],
        out_specs=[pl.BlockSpec((1, tl, ROW_SLABS, LANES), lambda b, t: (b, t, 0, 0)),
                   pl.BlockSpec((1, tl, LANES), tok)],
        out_shape=[jax.ShapeDtypeStruct((bsz, seq, ROW_SLABS, LANES), F32),
                   jax.ShapeDtypeStruct((bsz, seq, LANES), F32)],
        compiler_params=_params(("parallel", "parallel")),
        name="moe_route",
    )(h, mod2, ng, wr, br)


def _moe_schedule(route, tm):
    t = route.shape[0]
    a = 2 * t
    n_tiles = a // tm + N_EXPERTS
    ea = route[:, :2].astype(jnp.int32).reshape(a)
    _, order = lax.sort((ea, jnp.arange(a, dtype=jnp.int32)), num_keys=1, is_stable=True)
    counts = jnp.sum((ea[:, None] == jnp.arange(N_EXPERTS, dtype=jnp.int32)[None, :]).astype(jnp.int32), axis=0)
    padded = (counts + tm - 1) // tm * tm
    pad_end = jnp.cumsum(padded)
    pad_start = pad_end - padded
    start = jnp.cumsum(counts) - counts
    tile_first = jnp.arange(n_tiles, dtype=jnp.int32) * tm
    tile_e = jnp.minimum(jnp.sum((tile_first[:, None] >= pad_end[None, :]).astype(jnp.int32), axis=1),
                         N_EXPERTS - 1)
    tile_valid = (tile_first < pad_end[-1]).astype(jnp.int32)
    row = jnp.arange(tm, dtype=jnp.int32)[None, :]
    local = (tile_first - pad_start[tile_e])[:, None] + row
    valid = local < counts[tile_e][:, None]
    asg = order[jnp.clip(start[tile_e][:, None] + local, 0, a - 1)]
    tok = jnp.where(valid, asg // 2, 0)
    dump = 2 * t + (jnp.arange(n_tiles, dtype=jnp.int32) % 2)[:, None] * tm + row
    dest = jnp.where(valid, (asg % 2) * t + asg // 2, dump)
    return tile_e, tile_valid, tok.reshape(n_tiles, 1, tm), dest.reshape(n_tiles, 1, tm)


def _moe_ffn_kernel(te_ref, tv_ref, tok_ref, tokn_ref, dest_ref, w1_ref, w3_ref, w2_ref, u_hbm, y_hbm,
                    xbuf, ybuf, w1b, w3b, w2b, gsem, ssem, *, tm, n_tiles):
    i = pl.program_id(0)
    slot = lax.rem(i, 2)

    def gather_rows(idx_ref, s):
        def body(r, c):
            pltpu.make_async_copy(u_hbm.at[pl.ds(idx_ref[0, 0, r], 1)], xbuf.at[s, pl.ds(r, 1)],
                                  gsem.at[s]).start()
            return c
        lax.fori_loop(0, tm, body, 0, unroll=8)

    def wait_gather(s):
        pltpu.make_async_copy(u_hbm.at[pl.ds(0, tm)], xbuf.at[s], gsem.at[s]).wait()

    def wait_scatter(s):
        pltpu.make_async_copy(ybuf.at[s], y_hbm.at[pl.ds(0, tm)], ssem.at[s]).wait()

    @pl.when(i == 0)
    def _():
        ybuf[...] = jnp.zeros(ybuf.shape, F32)
        n_real = y_hbm.shape[0] - 2 * tm
        for s in range(2):
            fill = pltpu.make_async_copy(ybuf.at[s], y_hbm.at[pl.ds(n_real + s * tm, tm)], ssem.at[s])
            fill.start()
            fill.wait()

    @pl.when((i == 0) & (tv_ref[0] == 1))
    def _():
        gather_rows(tok_ref, 0)

    @pl.when((i + 1 < n_tiles) & (tv_ref[jnp.minimum(i + 1, n_tiles - 1)] == 1))
    def _():
        gather_rows(tokn_ref, 1 - slot)

    @pl.when((i >= 2) & (tv_ref[jnp.maximum(i - 2, 0)] == 1))
    def _():
        wait_scatter(slot)

    @pl.when(tv_ref[i] == 1)
    def _():
        @pl.when((i == 0) | (te_ref[i] != te_ref[jnp.maximum(i - 1, 0)]))
        def _():
            w1b[...] = w1_ref[0, 0].astype(BF16)
            w3b[...] = w3_ref[0, 0].astype(BF16)
            w2b[...] = w2_ref[0, 0].astype(BF16)

        wait_gather(slot)
        x = _load_row_slabs(xbuf.at[slot]).astype(BF16)
        hid = _silu(jnp.dot(x, w1b[...], preferred_element_type=F32)) * jnp.dot(
            x, w3b[...], preferred_element_type=F32)
        _store_row_slabs(ybuf.at[slot], jnp.dot(hid.astype(BF16), w2b[...], preferred_element_type=F32))

        def body(r, c):
            pltpu.make_async_copy(ybuf.at[slot, pl.ds(r, 1)], y_hbm.at[pl.ds(dest_ref[0, 0, r], 1)],
                                  ssem.at[slot]).start()
            return c
        lax.fori_loop(0, tm, body, 0, unroll=8)

    @pl.when(i == n_tiles - 1)
    def _():
        @pl.when(tv_ref[n_tiles - 2] == 1)
        def _():
            wait_scatter(1 - slot)

        @pl.when(tv_ref[n_tiles - 1] == 1)
        def _():
            wait_scatter(slot)


def _moe_ffn(u2d, tile_e, tile_valid, tok, dest, w1, w3, w2, layer, tm):
    t = u2d.shape[0]
    n_tiles = tok.shape[0]
    smem_tile = lambda f: pl.BlockSpec((1, 1, tm), f, memory_space=pltpu.SMEM)
    wspec = lambda shp: pl.BlockSpec((1,) + shp, lambda i, te, tv: (layer, te[i], 0, 0))
    grid_spec = pltpu.PrefetchScalarGridSpec(
        num_scalar_prefetch=2,
        grid=(n_tiles,),
        in_specs=[
            smem_tile(lambda i, te, tv: (i, 0, 0)),
            smem_tile(lambda i, te, tv: (jnp.minimum(i + 1, n_tiles - 1), 0, 0)),
            smem_tile(lambda i, te, tv: (i, 0, 0)),
            wspec((1, D_MODEL, D_EXPERT)),
            wspec((1, D_MODEL, D_EXPERT)),
            wspec((1, D_EXPERT, D_MODEL)),
            pl.BlockSpec(memory_space=pl.ANY),
        ],
        out_specs=pl.BlockSpec(memory_space=pl.ANY),
        scratch_shapes=[
            pltpu.VMEM((2, tm, ROW_SLABS, LANES), F32),
            pltpu.VMEM((2, tm, ROW_SLABS, LANES), F32),
            pltpu.VMEM((D_MODEL, D_EXPERT), BF16),
            pltpu.VMEM((D_MODEL, D_EXPERT), BF16),
            pltpu.VMEM((D_EXPERT, D_MODEL), BF16),
            pltpu.SemaphoreType.DMA((2,)),
            pltpu.SemaphoreType.DMA((2,)),
        ],
    )
    return pl.pallas_call(
        functools.partial(_moe_ffn_kernel, tm=tm, n_tiles=n_tiles),
        grid_spec=grid_spec,
        out_shape=jax.ShapeDtypeStruct((2 * t + 2 * tm, ROW_SLABS, LANES), F32),
        compiler_params=_params(("arbitrary",)),
        name="moe_ffn",
    )(tile_e, tile_valid, tok, tok, dest, w1, w3, w2, u2d)


def _moe_combine_kernel(h_ref, gate_ref, route_ref, ya_ref, yb_ref, o_ref):
    route = route_ref[0]
    lane = lax.broadcasted_iota(jnp.int32, route.shape, 1)
    wa = jnp.sum(jnp.where(lane == 2, route, 0.0), axis=-1, keepdims=True)
    wb = jnp.sum(jnp.where(lane == 3, route, 0.0), axis=-1, keepdims=True)
    o_ref[0] = h_ref[0] + gate_ref[0] * (wa * _load_row_slabs(ya_ref) + wb * _load_row_slabs(yb_ref))


def _moe_combine(h, gate, route, y2, tl):
    bsz, seq, _ = h.shape
    nt = seq // tl
    tok = lambda b, t: (b, t, 0)
    return pl.pallas_call(
        _moe_combine_kernel,
        grid=(bsz, nt),
        in_specs=[
            pl.BlockSpec((1, tl, D_MODEL), tok),
            pl.BlockSpec((1, 1, D_MODEL), lambda b, t: (b, 0, 0)),
            pl.BlockSpec((1, tl, LANES), tok),
            pl.BlockSpec((tl, ROW_SLABS, LANES), lambda b, t: (b * nt + t, 0, 0)),
            pl.BlockSpec((tl, ROW_SLABS, LANES), lambda b, t: (bsz * nt + b * nt + t, 0, 0)),
        ],
        out_specs=pl.BlockSpec((1, tl, D_MODEL), tok),
        out_shape=jax.ShapeDtypeStruct((bsz, seq, D_MODEL), F32),
        compiler_params=_params(("parallel", "parallel")),
        name="moe_combine",
    )(h, gate, route, y2, y2)


def _alibi_slopes():
    start = 2.0 ** (-8.0 / DIFF_HEADS)
    return jnp.asarray([start ** (i + 1) for i in range(DIFF_HEADS)], F32)


def _head_expand_matrix():
    head_of_lane = jnp.arange(D_INNER) // SSM_HEAD_DIM
    e = (jnp.arange(DT_PAD)[:, None] == head_of_lane[None, :]).astype(BF16)
    return jnp.concatenate([e, e], axis=0)


def _pad_lanes(x, width):
    return jnp.pad(x, [(0, 0)] * (x.ndim - 1) + [(0, width - x.shape[-1])])


def _prep_weights(p):
    w = {}
    w_in = p["ssm_w_in"]
    w["ssm_w_in"] = jnp.concatenate(
        [w_in[..., :D_INNER + CONV_DIM], _pad_lanes(w_in[..., D_INNER + CONV_DIM:], DT_PAD)], axis=-1).astype(BF16)
    w["ssm_dt_bias"] = _pad_lanes(p["ssm_dt_bias"], DT_PAD)[:, None, :]
    w["ssm_a_log"] = _pad_lanes(p["ssm_a_log"], DT_PAD)[:, None, :]
    w["ssm_d"] = jnp.repeat(p["ssm_d"], SSM_HEAD_DIM, axis=-1)[:, None, :]
    w["ssm_w_out"] = p["ssm_w_out"].astype(BF16)
    w["e2"] = _head_expand_matrix()
    w["w_kv"] = p["w_kv"].astype(BF16)
    w["k_norm_g"] = jnp.tile(p["k_norm_g"], 2 * DIFF_HEADS)[None, :]
    w["attn_w_q"] = p["attn_w_q"].astype(BF16)
    w["attn_q_norm_g"] = jnp.tile(p["attn_q_norm_g"], (1, 2 * DIFF_HEADS))[:, None, :]
    lam = jnp.stack([p["attn_lambda_q1"], p["attn_lambda_k1"], p["attn_lambda_q2"], p["attn_lambda_k2"]], axis=1)
    w["lam_rows"] = jnp.pad(lam, ((0, 0), (0, 4), (0, LANES - DIFF_HEAD_DIM)))
    w["attn_w_o"] = p["attn_w_o"].astype(BF16)
    wr = jnp.concatenate([_pad_lanes(p["moe_w_group"], LANES), _pad_lanes(p["moe_w_expert"], LANES)], axis=-1)
    br = jnp.concatenate([_pad_lanes(p["moe_b_group"], LANES), _pad_lanes(p["moe_b_expert"], LANES)], axis=-1)
    w["moe_wr"] = wr.astype(BF16)
    w["moe_br"] = br[:, None, :]
    return w


def _trunk(x, mods, kv_mod, conv_prev, ssm_prev, k_past, v_past, p, w, cfg):
    bsz, seq, _ = x.shape
    tl, lt, lc, tq, tk = cfg["tl"], cfg["lt"], cfg["lc"], cfg["tq"], cfg["tk"]
    pos0 = 0 if k_past is None else k_past.shape[1]
    h = x
    conv_new, ssm_new = [], []
    k_new = v_new = k_all = kt_all = v_all = None
    for layer in range(DEPTH):
        mod = mods[layer]
        if layer < N_A_LAYERS:
            i = layer
            cprev = jnp.pad(conv_prev[i], ((0, 0), (8 - (CONV_WIDTH - 1), 0), (0, 0)))
            sprev = ssm_prev[i].reshape(bsz, D_INNER, SSM_STATE)
            h, cn, sn = _mamba_layer(
                h, mod, p["norm_g"][layer, 0][None, :], w["ssm_w_in"][i], p["ssm_conv_w"][i],
                p["ssm_conv_b"][i][None, :], w["ssm_dt_bias"][i], w["ssm_a_log"][i], w["ssm_d"][i],
                p["ssm_norm_g"][i][None, :], w["ssm_w_out"][i], cprev, sprev, w["e2"], lt, lc)
            conv_new.append(cn[:, 8 - (CONV_WIDTH - 1):, :])
            ssm_new.append(sn.reshape(bsz, SSM_HEADS, SSM_HEAD_DIM, SSM_STATE))
        else:
            j = layer - N_A_LAYERS
            lam_init = 0.8 - 0.6 * math.exp(-0.3 * layer)
            q = _normproj(h, mod[:, 0:2], p["norm_g"][layer, 0][None, :], w["attn_w_q"][j],
                          w["attn_q_norm_g"][j], tl, False, "attn_q_proj")[0]
            slopes = _alibi_slopes()
            sub_g = p["attn_sub_g"][j][None, :]
            online = functools.partial(_diff_attention, slopes=slopes, lam_rows=w["lam_rows"][j], sub_g=sub_g,
                                       tq=tq, tk=tk, pos0=pos0, lam_init=lam_init)
            if k_past is None and tq == tk:
                bound = (DIFF_HEAD_DIM ** 0.5 * (1.0 + 2.0 ** -6) * jnp.max(jnp.abs(p["attn_q_norm_g"][j]))
                         * jnp.max(jnp.abs(p["k_norm_g"])))
                slopes_shift = jnp.concatenate([slopes, bound[None], jnp.zeros((7,), F32)])
                o = lax.cond(
                    bound <= FIXED_SHIFT_LIMIT,
                    lambda q_, k_, kt_, v_: _diff_attention_fixed(q_, kt_, v_, slopes_shift, w["lam_rows"][j],
                                                                  sub_g, tq, lam_init),
                    lambda q_, k_, kt_, v_: online(q_, k_, v_),
                    q, k_all, kt_all, v_all)
            else:
                o = online(q, k_all, v_all)
            h = _proj_res(o, w["attn_w_o"][j], h, mod[:, 2:3], tl)
        u, route = _moe_route(h, mod[:, 3:5], p["norm_g"][layer, 1][None, :], w["moe_wr"][layer],
                              w["moe_br"][layer], tl)
        tile_e, tile_valid, tok, dest = _moe_schedule(route.reshape(bsz * seq, LANES), cfg["tm"])
        y2 = _moe_ffn(u.reshape(bsz * seq, ROW_SLABS, LANES), tile_e, tile_valid, tok, dest, p["moe_w1"],
                      p["moe_w3"], p["moe_w2"], layer, cfg["tm"])
        h = _moe_combine(h, mod[:, 5:6], route, y2, tl)
        if layer == N_A_LAYERS - 1:
            k_f32, v_f32, k_all, v_all = _normproj(h, kv_mod, p["kv_norm_g"][None, :], w["w_kv"], w["k_norm_g"],
                                                   tl, True, "shared_kv")
            nk_cols = DIFF_HEADS * 2 * DIFF_HEAD_DIM
            k_new = k_f32.reshape(bsz, seq, DIFF_HEADS, 2, DIFF_HEAD_DIM)
            v_new = v_f32.reshape(bsz, seq, DIFF_HEADS, DIFF_V_DIM)
            if k_past is None and tq == tk:
                kt_all = _key_blocks_transposed(k_all, tk)
            if k_past is not None:
                k_all = jnp.concatenate([k_past.reshape(bsz, pos0, nk_cols).astype(BF16), k_all], axis=1)
                v_all = jnp.concatenate([v_past.reshape(bsz, pos0, nk_cols).astype(BF16), v_all], axis=1)
                pad = (-k_all.shape[1]) % tk
                k_all = jnp.pad(k_all, ((0, 0), (0, pad), (0, 0)))
                v_all = jnp.pad(v_all, ((0, 0), (0, pad), (0, 0)))
    return h, jnp.stack(conv_new), jnp.stack(ssm_new), k_new, v_new


PROMPT_CFG = dict(tl=512, lt=256, lc=64, tq=512, tk=512, tm=256)
SAMPLE_CFG = dict(tl=64, lt=64, lc=64, tq=64, tk=1152, tm=64)


def kernel(x_prompt, x_sample, c_prompt, c_sample, state_conv, state_ssm, cache_k, cache_v, ada_w, ada_b, norm_g, ssm_w_in, ssm_conv_w, ssm_conv_b, ssm_dt_bias, ssm_a_log, ssm_d, ssm_norm_g, ssm_w_out, kv_norm_g, kv_ada_w, kv_ada_b, w_kv, k_norm_g, attn_w_q, attn_q_norm_g, attn_lambda_q1, attn_lambda_k1, attn_lambda_q2, attn_lambda_k2, attn_sub_g, attn_w_o, moe_w_group, moe_b_group, moe_w_expert, moe_b_expert, moe_w1, moe_w3, moe_w2):
    p = dict(norm_g=norm_g, ssm_w_in=ssm_w_in, ssm_conv_w=ssm_conv_w, ssm_conv_b=ssm_conv_b,
             ssm_dt_bias=ssm_dt_bias, ssm_a_log=ssm_a_log, ssm_d=ssm_d, ssm_norm_g=ssm_norm_g,
             ssm_w_out=ssm_w_out, kv_norm_g=kv_norm_g, w_kv=w_kv, k_norm_g=k_norm_g, attn_w_q=attn_w_q,
             attn_q_norm_g=attn_q_norm_g, attn_lambda_q1=attn_lambda_q1, attn_lambda_k1=attn_lambda_k1,
             attn_lambda_q2=attn_lambda_q2, attn_lambda_k2=attn_lambda_k2, attn_sub_g=attn_sub_g,
             attn_w_o=attn_w_o, moe_w_group=moe_w_group, moe_b_group=moe_b_group, moe_w_expert=moe_w_expert,
             moe_b_expert=moe_b_expert, moe_w1=moe_w1, moe_w3=moe_w3, moe_w2=moe_w2)
    w = _prep_weights(p)

    bp, bs = x_prompt.shape[0], x_sample.shape[0]
    c_all = jnp.concatenate([c_prompt, c_sample], axis=0)
    c_all = jnp.pad(c_all, ((0, (-c_all.shape[0]) % 8), (0, 0)))
    mods = _mods(c_all, ada_w, ada_b[:, None, :], 1536).reshape(DEPTH, -1, 6, D_MODEL)
    kv_mods = _mods(c_all, kv_ada_w[None], kv_ada_b[None, None, :], 1024).reshape(-1, 2, D_MODEL)

    conv0 = jnp.zeros((N_A_LAYERS, bp, CONV_WIDTH - 1, CONV_DIM), F32)
    ssm0 = jnp.zeros((N_A_LAYERS, bp, SSM_HEADS, SSM_HEAD_DIM, SSM_STATE), F32)
    y_p, conv_p, ssm_p, k_p, v_p = _trunk(x_prompt, mods[:, :bp], kv_mods[:bp], conv0, ssm0, None, None,
                                          p, w, PROMPT_CFG)
    y_s, conv_s, ssm_s, k_s, v_s = _trunk(x_sample, mods[:, bp:bp + bs], kv_mods[bp:bp + bs], state_conv,
                                          state_ssm, cache_k, cache_v, p, w, SAMPLE_CFG)
    return (y_p, y_s, conv_p, ssm_p, k_p, v_p, conv_s, ssm_s, k_s, v_s)
```
